```python
import jax, jax.numpy as jnp
from jax import lax
import numpy as np

D_MODEL = 1024
BATCH = 16
SEQ = 2048
DEPTH = 2

MIX_WIDTH = D_MODEL
CONV_WIDTH = MIX_WIDTH // 2
SGU_WIDTH = MIX_WIDTH - CONV_WIDTH
HEAD_DIM = 64
CONV_HEADS = CONV_WIDTH // HEAD_DIM
SGU_HEADS = SGU_WIDTH // HEAD_DIM
CONV_K = 3
CHUNK = 128
IN_COLS = 3 * CONV_WIDTH + 2 * SGU_WIDTH
D_FF = ((8 * D_MODEL // 3 + 255) // 256) * 256
N_MOD = 6
EPS = 1e-6

kernel_name = "hybrid_shortconv_chunked_sgu_adaln"


def rmsnorm(x, g):
    xf = x.astype(jnp.float32)
    xf = xf * lax.rsqrt(jnp.mean(xf * xf, axis=-1, keepdims=True) + EPS)
    return xf.astype(x.dtype) * g


def short_conv_mixer(bg, cg, h, conv_w):
    z = cg * h
    zp = jnp.pad(z, ((0, 0), (CONV_K - 1, 0), (0, 0)))
    s = z.shape[1]
    conv = sum(conv_w[k] * zp[:, k:k + s, :] for k in range(CONV_K))
    return bg * conv


def chunked_sgu_mixer(u, v, v_norm, w_s, b_s):
    bsz, s, _ = u.shape
    n_chunks = s // CHUNK
    v = rmsnorm(v, v_norm)
    vc = v.reshape(bsz, n_chunks, CHUNK, SGU_HEADS, HEAD_DIM)
    mask = jnp.tril(jnp.ones((CHUNK, CHUNK), dtype=w_s.dtype))
    ws = w_s * mask[None]
    mixed = jnp.einsum('hij,bcjhd->bcihd', ws, vc)
    mixed = mixed + jnp.transpose(b_s)[None, None, :, :, None]
    return u * mixed.reshape(bsz, s, SGU_WIDTH)


def setup_inputs(seed: int = 0) -> dict:
    key = jax.random.key(seed)
    ks = jax.random.split(key, 20)
    f32 = jnp.float32
    nrm = lambda k, shape, scale: jax.random.normal(k, shape, f32) * scale
    gain = lambda k, shape: 1.0 + 0.02 * jax.random.normal(k, shape, f32)
    return {
        "x": nrm(ks[0], (BATCH, SEQ, D_MODEL), 1.0),
        "c": nrm(ks[1], (BATCH, D_MODEL), 1.0),
        "w_mod": nrm(ks[2], (DEPTH, D_MODEL, N_MOD * D_MODEL), D_MODEL ** -0.5),
        "b_mod": nrm(ks[3], (DEPTH, N_MOD * D_MODEL), 0.01),
        "norm_mix": gain(ks[4], (DEPTH, D_MODEL)),
        "w_in": nrm(ks[5], (DEPTH, D_MODEL, IN_COLS), D_MODEL ** -0.5),
        "conv_w": nrm(ks[6], (DEPTH, CONV_K, CONV_WIDTH), CONV_K ** -0.5),
        "v_norm": gain(ks[7], (DEPTH, SGU_WIDTH)),
        "w_s": nrm(ks[8], (DEPTH, SGU_HEADS, CHUNK, CHUNK), CHUNK ** -0.5),
        "b_s": gain(ks[9], (DEPTH, SGU_HEADS, CHUNK)),
        "out_norm_a": gain(ks[10], (DEPTH, CONV_WIDTH)),
        "out_norm_b": gain(ks[11], (DEPTH, SGU_WIDTH)),
        "w_out": nrm(ks[12], (DEPTH, MIX_WIDTH, D_MODEL), MIX_WIDTH ** -0.5),
        "norm_ffn": gain(ks[13], (DEPTH, D_MODEL)),
        "w_up": nrm(ks[14], (DEPTH, D_MODEL, 2 * D_FF), D_MODEL ** -0.5),
        "w_down": nrm(ks[15], (DEPTH, D_FF, D_MODEL), D_FF ** -0.5),
        "norm_final": gain(ks[16], (D_MODEL,)),
    }


def reference(x, c, w_mod, b_mod, norm_mix, w_in, conv_w, v_norm, w_s, b_s,
              out_norm_a, out_norm_b, w_out, norm_ffn, w_up, w_down, norm_final):
    c_act = jax.nn.silu(c)
    for l in range(DEPTH):
        mod = c_act @ w_mod[l] + b_mod[l]
        sh_m, sc_m, g_m, sh_f, sc_f, g_f = [m[:, None, :] for m in jnp.split(mod, N_MOD, axis=-1)]

        h = rmsnorm(x, norm_mix[l]) * (1.0 + sc_m) + sh_m
        proj = h @ w_in[l]
        bg, cg, hc, u, v = jnp.split(
            proj, np.cumsum([CONV_WIDTH, CONV_WIDTH, CONV_WIDTH, SGU_WIDTH])[:4].tolist(), axis=-1)
        y_a = short_conv_mixer(bg, cg, hc, conv_w[l])
        y_b = chunked_sgu_mixer(jax.nn.gelu(u), jax.nn.gelu(v), v_norm[l], w_s[l], b_s[l])
        y_mix = jnp.concatenate([rmsnorm(y_a, out_norm_a[l]), rmsnorm(y_b, out_norm_b[l])], axis=-1)
        x = x + g_m * (y_mix @ w_out[l])

        h = rmsnorm(x, norm_ffn[l]) * (1.0 + sc_f) + sh_f
        gate, up = jnp.split(h @ w_up[l], 2, axis=-1)
        x = x + g_f * ((jax.nn.silu(gate) * up) @ w_down[l])

    return rmsnorm(x, norm_final)
```

```python
import functools

import jax
import jax.numpy as jnp
from jax import lax
from jax.experimental import pallas as pl
from jax.experimental.pallas import tpu as pltpu

EPS = 1e-6
HEAD_DIM = 64
CHUNK = 128
CONV_K = 3
N_MOD = 6
LANES = 128
SUBLANES = 8
TOKEN_TILE = 512
FF_CHUNK = 256
VMEM_LIMIT_BYTES = 56 * 1024 * 1024

f32 = jnp.float32
bf16 = jnp.bfloat16


def _rms(x):
    return x * lax.rsqrt(jnp.mean(x * x, axis=-1, keepdims=True) + EPS)


def _dot(a, b):
    return jnp.dot(a, b, preferred_element_type=f32)


def _mod_kernel(c_ref, w_ref, b_ref, o_ref):
    c = c_ref[...]
    o_ref[...] = _dot(c * jax.nn.sigmoid(c), w_ref[...]) + b_ref[...]


def _modulation(c, w_mod, b_mod):
    depth, d, n = w_mod.shape
    bsz = c.shape[0]
    nblk = n // d
    return pl.pallas_call(
        _mod_kernel,
        grid=(depth, nblk),
        in_specs=[
            pl.BlockSpec((bsz, d), lambda l, j: (0, 0)),
            pl.BlockSpec((None, d, d), lambda l, j: (l, 0, j)),
            pl.BlockSpec((None, 1, d), lambda l, j: (l, 0, j)),
        ],
        out_specs=pl.BlockSpec((None, bsz, d), lambda l, j: (l, 0, j)),
        out_shape=jax.ShapeDtypeStruct((depth, bsz, n), f32),
        compiler_params=pltpu.CompilerParams(
            dimension_semantics=("arbitrary", "arbitrary")),
        name="modulation",
    )(c, w_mod, b_mod.reshape(depth, 1, n))


def _mixer_kernel(x_ref, mod_ref, nmix_ref, win_ref, convw_ref, vnorm_ref,
                  ws_ref, bias_ref, ona_ref, onb_ref, wout_ref, o_ref,
                  carry_ref, wsk_ref, *, tiles_per_seq):
    i = pl.program_id(0)
    tm = x_ref.shape[0]
    cw = convw_ref.shape[1]
    sw = vnorm_ref.shape[1]
    n_pairs = wsk_ref.shape[0]

    @pl.when(i == 0)
    def _():
        r = lax.broadcasted_iota(jnp.int32, (CHUNK, CHUNK), 0)
        col = lax.broadcasted_iota(jnp.int32, (CHUNK, CHUNK), 1)
        tril = col <= r
        for p in range(n_pairs):
            a = jnp.where(tril, ws_ref[2 * p], 0.0)
            b = jnp.where(tril, ws_ref[2 * p + 1], 0.0)
            wsk_ref[p] = jnp.concatenate([a, b], axis=1).astype(bf16)

    @pl.when(i % tiles_per_seq == 0)
    def _():
        carry_ref[...] = jnp.zeros_like(carry_ref)

    x = x_ref[...]
    sh = mod_ref[0:1, :]
    sc = mod_ref[1:2, :]
    gate = mod_ref[2:3, :]
    h = (_rms(x) * nmix_ref[...]) * (1.0 + sc) + sh
    proj = _dot(h.astype(bf16), win_ref[...])
    bg = proj[:, 0:cw]
    cg = proj[:, cw:2 * cw]
    hc = proj[:, 2 * cw:3 * cw]
    u = proj[:, 3 * cw:3 * cw + sw]
    v = proj[:, 3 * cw + sw:]

    z = cg * hc
    prev = carry_ref[...]
    carry_ref[...] = z[tm - SUBLANES:, :]
    w0 = convw_ref[0:1, :]
    w1 = convw_ref[1:2, :]
    w2 = convw_ref[2:3, :]
    r1 = pltpu.roll(z, 1, 0)
    r2 = pltpu.roll(z, 2, 0)
    conv = w2 * z + w1 * r1 + w0 * r2
    row = lax.broadcasted_iota(jnp.int32, (SUBLANES, cw), 0)
    z1h = jnp.where(row < 1, pltpu.roll(prev, 1, 0), r1[:SUBLANES])
    z2h = jnp.where(row < 2, pltpu.roll(prev, 2, 0), r2[:SUBLANES])
    conv_head = w2 * z[:SUBLANES] + w1 * z1h + w0 * z2h
    conv = jnp.concatenate([conv_head, conv[SUBLANES:]], axis=0)
    y_a = _rms(bg * conv) * ona_ref[...]

    ug = jax.nn.gelu(u)
    vn = (_rms(jax.nn.gelu(v)) * vnorm_ref[...]).astype(bf16)
    lane = lax.broadcasted_iota(jnp.int32, (CHUNK, LANES), 1)
    lo_mask = lane < HEAD_DIM
    zero = jnp.zeros((CHUNK, LANES), bf16)
    bias = bias_ref[...]
    mixed_rows = []
    for c in range(tm // CHUNK):
        blocks = []
        for p in range(n_pairs):
            vb = vn[c * CHUNK:(c + 1) * CHUNK, p * LANES:(p + 1) * LANES]
            rhs = jnp.concatenate(
                [jnp.where(lo_mask, vb, zero), jnp.where(lo_mask, zero, vb)], axis=0)
            blocks.append(_dot(wsk_ref[p], rhs))
        mixed_rows.append(jnp.concatenate(blocks, axis=1) + bias)
    mixed = jnp.concatenate(mixed_rows, axis=0)
    y_b = _rms(ug * mixed) * onb_ref[...]

    y = (_dot(y_a.astype(bf16), wout_ref[0:cw, :])
         + _dot(y_b.astype(bf16), wout_ref[cw:, :]))
    o_ref[...] = x + gate * y


def _mixer(x2d, mod_l, nmix, win, convw, vnorm, ws, bias, ona, onb, wout, *, seq):
    n_tok, d = x2d.shape
    tm = TOKEN_TILE
    tiles_per_seq = seq // tm
    cw = convw.shape[1]
    sw = vnorm.shape[1]
    n_heads = ws.shape[0]
    const = lambda *shape: pl.BlockSpec(shape, lambda i: (0,) * len(shape),
                                        pipeline_mode=pl.Buffered(1))
    return pl.pallas_call(
        functools.partial(_mixer_kernel, tiles_per_seq=tiles_per_seq),
        grid=(n_tok // tm,),
        in_specs=[
            pl.BlockSpec((tm, d), lambda i: (i, 0)),
            pl.BlockSpec((None, N_MOD, d), lambda i: (i // tiles_per_seq, 0, 0)),
            const(1, d),
            const(*win.shape),
            const(CONV_K, cw),
            const(1, sw),
            const(n_heads, CHUNK, CHUNK),
            const(CHUNK, sw),
            const(1, cw),
            const(1, sw),
            const(*wout.shape),
        ],
        out_specs=pl.BlockSpec((tm, d), lambda i: (i, 0)),
        out_shape=jax.ShapeDtypeStruct((n_tok, d), f32),
        scratch_shapes=[
            pltpu.VMEM((SUBLANES, cw), f32),
            pltpu.VMEM((n_heads // 2, CHUNK, 2 * CHUNK), bf16),
        ],
        compiler_params=pltpu.CompilerParams(
            dimension_semantics=("arbitrary",),
            vmem_limit_bytes=VMEM_LIMIT_BYTES),
        name="mixer",
    )(x2d, mod_l, nmix, win, convw, vnorm, ws, bias, ona, onb, wout)


def _ffn_kernel(x_ref, mod_ref, nffn_ref, wup_ref, wdown_ref, nfin_ref, o_ref,
                *, final_norm):
    d_ff = wdown_ref.shape[0]
    x = x_ref[...]
    sh = mod_ref[3:4, :]
    sc = mod_ref[4:5, :]
    gate = mod_ref[5:6, :]
    h = ((_rms(x) * nffn_ref[...]) * (1.0 + sc) + sh).astype(bf16)
    acc = None
    for c in range(d_ff // FF_CHUNK):
        lo = c * FF_CHUNK
        g = _dot(h, wup_ref[:, lo:lo + FF_CHUNK])
        up = _dot(h, wup_ref[:, d_ff + lo:d_ff + lo + FF_CHUNK])
        a = (g * jax.nn.sigmoid(g) * up).astype(bf16)
        part = _dot(a, wdown_ref[lo:lo + FF_CHUNK, :])
        acc = part if acc is None else acc + part
    y = x + gate * acc
    if final_norm:
        y = _rms(y) * nfin_ref[...]
    o_ref[...] = y


def _ffn(x2d, mod_l, nffn, wup, wdown, nfin, *, seq, final_norm):
    n_tok, d = x2d.shape
    tm = TOKEN_TILE
    tiles_per_seq = seq // tm
    const = lambda *shape: pl.BlockSpec(shape, lambda i: (0,) * len(shape),
                                        pipeline_mode=pl.Buffered(1))
    return pl.pallas_call(
        functools.partial(_ffn_kernel, final_norm=final_norm),
        grid=(n_tok // tm,),
        in_specs=[
            pl.BlockSpec((tm, d), lambda i: (i, 0)),
            pl.BlockSpec((None, N_MOD, d), lambda i: (i // tiles_per_seq, 0, 0)),
            const(1, d),
            const(*wup.shape),
            const(*wdown.shape),
            const(1, d),
        ],
        out_specs=pl.BlockSpec((tm, d), lambda i: (i, 0)),
        out_shape=jax.ShapeDtypeStruct((n_tok, d), f32),
        compiler_params=pltpu.CompilerParams(
            dimension_semantics=("arbitrary",),
            vmem_limit_bytes=VMEM_LIMIT_BYTES),
        name="ffn_final" if final_norm else "ffn",
    )(x2d, mod_l, nffn, wup, wdown, nfin)


def kernel(x, c, w_mod, b_mod, norm_mix, w_in, conv_w, v_norm, w_s, b_s,
           out_norm_a, out_norm_b, w_out, norm_ffn, w_up, w_down, norm_final):
    bsz, seq, d = x.shape
    depth = w_mod.shape[0]
    assert seq % TOKEN_TILE == 0 and TOKEN_TILE % CHUNK == 0
    assert w_down.shape[1] % FF_CHUNK == 0

    mod = _modulation(c, w_mod, b_mod).reshape(depth, bsz, N_MOD, d)
    bias = jnp.repeat(jnp.swapaxes(b_s, 1, 2), HEAD_DIM, axis=2)

    xt = x.reshape(bsz * seq, d)
    row = lambda p: p.reshape(1, -1)
    for l in range(depth):
        xt = _mixer(xt, mod[l], row(norm_mix[l]), w_in[l].astype(bf16), conv_w[l],
                    row(v_norm[l]), w_s[l], bias[l], row(out_norm_a[l]),
                    row(out_norm_b[l]), w_out[l].astype(bf16), seq=seq)
        xt = _ffn(xt, mod[l], row(norm_ffn[l]), w_up[l].astype(bf16),
                  w_down[l].astype(bf16), row(norm_final), seq=seq,
                  final_norm=(l == depth - 1))
    return xt.reshape(bsz, seq, d)
```

```python
import functools

import jax
import jax.numpy as jnp
from jax import lax
from jax.experimental import pallas as pl
from jax.experimental.pallas import tpu as pltpu

EPS = 1e-6
HEAD_DIM = 64
CHUNK = 128
CONV_K = 3
N_MOD = 6
LANES = 128
SUBLANES = 8
ROW_GROUP = 512
GROUPS_PER_STEP = 2
TOKEN_TILE = ROW_GROUP * GROUPS_PER_STEP
FF_CHUNK = 256
VMEM_LIMIT_BYTES = 56 * 1024 * 1024

f32 = jnp.float32
bf16 = jnp.bfloat16


def _rms(x):
    return x * lax.rsqrt(jnp.mean(x * x, axis=-1, keepdims=True) + EPS)


def _dot(a, b):
    return jnp.dot(a, b, preferred_element_type=f32)


def _const_spec(shape, index):
    return pl.BlockSpec(shape, lambda i: index, pipeline_mode=pl.Buffered(1))


def _mod_kernel(c_ref, w_ref, b_ref, o_ref):
    c = c_ref[...]
    o_ref[...] = _dot(c * jax.nn.sigmoid(c), w_ref[...]) + b_ref[...]


def _modulation(c, w_mod, b_mod):
    depth, d, n = w_mod.shape
    bsz = c.shape[0]
    nblk = n // d
    return pl.pallas_call(
        _mod_kernel,
        grid=(depth, nblk),
        in_specs=[
            pl.BlockSpec((bsz, d), lambda l, j: (0, 0)),
            pl.BlockSpec((None, d, d), lambda l, j: (l, 0, j)),
            pl.BlockSpec((None, 1, d), lambda l, j: (l, 0, j)),
        ],
        out_specs=pl.BlockSpec((None, bsz, d), lambda l, j: (l, 0, j)),
        out_shape=jax.ShapeDtypeStruct((depth, bsz, n), f32),
        compiler_params=pltpu.CompilerParams(
            dimension_semantics=("arbitrary", "arbitrary")),
        name="modulation",
    )(c, w_mod, b_mod.reshape(depth, 1, n))


def _mixer_kernel(x_ref, mod_ref, nmix_ref, win_ref, convw_ref, vnorm_ref,
                  ws_ref, bias_ref, ona_ref, onb_ref, wout_ref, o_ref,
                  carry_ref, wsk_ref, *, tiles_per_seq):
    i = pl.program_id(0)
    cw = convw_ref.shape[1]
    sw = vnorm_ref.shape[1]
    n_pairs = wsk_ref.shape[0]
    rg = ROW_GROUP

    @pl.when(i == 0)
    def _():
        r = lax.broadcasted_iota(jnp.int32, (CHUNK, CHUNK), 0)
        col = lax.broadcasted_iota(jnp.int32, (CHUNK, CHUNK), 1)
        tril = col <= r
        for p in range(n_pairs):
            a = jnp.where(tril, ws_ref[2 * p], 0.0)
            b = jnp.where(tril, ws_ref[2 * p + 1], 0.0)
            wsk_ref[p] = jnp.concatenate([a, b], axis=1).astype(bf16)

    @pl.when(i % tiles_per_seq == 0)
    def _():
        carry_ref[...] = jnp.zeros_like(carry_ref)

    sh = mod_ref[0:1, :]
    sc = mod_ref[1:2, :]
    gate = mod_ref[2:3, :]
    w0 = convw_ref[0:1, :]
    w1 = convw_ref[1:2, :]
    w2 = convw_ref[2:3, :]
    row8 = lax.broadcasted_iota(jnp.int32, (SUBLANES, cw), 0)
    lo_mask = lax.broadcasted_iota(jnp.int32, (CHUNK, LANES), 1) < HEAD_DIM
    zero = jnp.zeros((CHUNK, LANES), bf16)

    def project(x):
        h = (_rms(x) * nmix_ref[...]) * (1.0 + sc) + sh
        return _dot(h.astype(bf16), win_ref[...])

    def conv_branch(proj, prev):
        bg = proj[:, 0:cw]
        z = proj[:, cw:2 * cw] * proj[:, 2 * cw:3 * cw]
        r1 = pltpu.roll(z, 1, 0)
        r2 = pltpu.roll(z, 2, 0)
        conv = w2 * z + w1 * r1 + w0 * r2
        z1h = jnp.where(row8 < 1, pltpu.roll(prev, 1, 0), r1[:SUBLANES])
        z2h = jnp.where(row8 < 2, pltpu.roll(prev, 2, 0), r2[:SUBLANES])
        conv_head = w2 * z[:SUBLANES] + w1 * z1h + w0 * z2h
        conv = jnp.concatenate([conv_head, conv[SUBLANES:]], axis=0)
        y_a = _rms(bg * conv) * ona_ref[...]
        return y_a.astype(bf16), z[rg - SUBLANES:, :]

    def sgu_branch(proj):
        ug = jax.nn.gelu(proj[:, 3 * cw:3 * cw + sw])
        vn = (_rms(jax.nn.gelu(proj[:, 3 * cw + sw:])) * vnorm_ref[...]).astype(bf16)
        bias = bias_ref[...]
        mixed_rows = []
        for c in range(rg // CHUNK):
            blocks = []
            for p in range(n_pairs):
                vb = vn[c * CHUNK:(c + 1) * CHUNK, p * LANES:(p + 1) * LANES]
                rhs = jnp.concatenate(
                    [jnp.where(lo_mask, vb, zero), jnp.where(lo_mask, zero, vb)], axis=0)
                blocks.append(_dot(wsk_ref[p], rhs))
            mixed_rows.append(jnp.concatenate(blocks, axis=1) + bias)
        mixed = jnp.concatenate(mixed_rows, axis=0)
        return (_rms(ug * mixed) * onb_ref[...]).astype(bf16)

    prev = carry_ref[...]
    xs = [x_ref[g * rg:(g + 1) * rg, :] for g in range(GROUPS_PER_STEP)]
    projs = [project(x) for x in xs]
    for g in range(GROUPS_PER_STEP):
        y_b = sgu_branch(projs[g])
        y_a, prev = conv_branch(projs[g], prev)
        y = _dot(y_b, wout_ref[cw:, :]) + _dot(y_a, wout_ref[0:cw, :])
        o_ref[g * rg:(g + 1) * rg, :] = xs[g] + gate * y
    carry_ref[...] = prev


def _mixer(x2d, mod, nmix, win, convw, vnorm, ws, bias, ona, onb, wout, *, layer, seq):
    n_tok, d = x2d.shape
    tm = TOKEN_TILE
    tiles_per_seq = seq // tm
    cw = convw.shape[-1]
    sw = vnorm.shape[-1]
    n_heads = ws.shape[1]
    l = layer
    return pl.pallas_call(
        functools.partial(_mixer_kernel, tiles_per_seq=tiles_per_seq),
        grid=(n_tok // tm,),
        in_specs=[
            pl.BlockSpec((tm, d), lambda i: (i, 0)),
            pl.BlockSpec((None, None, N_MOD, d), lambda i: (l, i // tiles_per_seq, 0, 0)),
            _const_spec((None, 1, d), (l, 0, 0)),
            _const_spec((None,) + win.shape[1:], (l, 0, 0)),
            _const_spec((None, CONV_K, cw), (l, 0, 0)),
            _const_spec((None, 1, sw), (l, 0, 0)),
            _const_spec((None, n_heads, CHUNK, CHUNK), (l, 0, 0, 0)),
            _const_spec((None, CHUNK, sw), (l, 0, 0)),
            _const_spec((None, 1, cw), (l, 0, 0)),
            _const_spec((None, 1, sw), (l, 0, 0)),
            _const_spec((None,) + wout.shape[1:], (l, 0, 0)),
        ],
        out_specs=pl.BlockSpec((tm, d), lambda i: (i, 0)),
        out_shape=jax.ShapeDtypeStruct((n_tok, d), f32),
        scratch_shapes=[
            pltpu.VMEM((SUBLANES, cw), f32),
            pltpu.VMEM((n_heads // 2, CHUNK, 2 * CHUNK), bf16),
        ],
        compiler_params=pltpu.CompilerParams(
            dimension_semantics=("arbitrary",),
            vmem_limit_bytes=VMEM_LIMIT_BYTES),
        name="mixer",
    )(x2d, mod, nmix, win, convw, vnorm, ws, bias, ona, onb, wout)


def _ffn_kernel(x_ref, mod_ref, nffn_ref, wup_ref, wdown_ref, nfin_ref, o_ref,
                *, final_norm):
    d_ff = wdown_ref.shape[0]
    rg = ROW_GROUP
    groups = range(GROUPS_PER_STEP)
    sh = mod_ref[3:4, :]
    sc = mod_ref[4:5, :]
    gate = mod_ref[5:6, :]
    xs = [x_ref[g * rg:(g + 1) * rg, :] for g in groups]
    hs = [((_rms(x) * nffn_ref[...]) * (1.0 + sc) + sh).astype(bf16) for x in xs]
    accs = [None for _ in groups]
    for c in range(d_ff // FF_CHUNK):
        lo = c * FF_CHUNK
        for g in groups:
            gt = _dot(hs[g], wup_ref[:, lo:lo + FF_CHUNK])
            up = _dot(hs[g], wup_ref[:, d_ff + lo:d_ff + lo + FF_CHUNK])
            a = (gt * jax.nn.sigmoid(gt) * up).astype(bf16)
            part = _dot(a, wdown_ref[lo:lo + FF_CHUNK, :])
            accs[g] = part if accs[g] is None else accs[g] + part
    for g in groups:
        y = xs[g] + gate * accs[g]
        if final_norm:
            y = _rms(y) * nfin_ref[...]
        o_ref[g * rg:(g + 1) * rg, :] = y


def _ffn(x2d, mod, nffn, wup, wdown, nfin, *, layer, seq, final_norm):
    n_tok, d = x2d.shape
    tm = TOKEN_TILE
    tiles_per_seq = seq // tm
    l = layer
    return pl.pallas_call(
        functools.partial(_ffn_kernel, final_norm=final_norm),
        grid=(n_tok // tm,),
        in_specs=[
            pl.BlockSpec((tm, d), lambda i: (i, 0)),
            pl.BlockSpec((None, None, N_MOD, d), lambda i: (l, i // tiles_per_seq, 0, 0)),
            _const_spec((None, 1, d), (l, 0, 0)),
            _const_spec((None,) + wup.shape[1:], (l, 0, 0)),
            _const_spec((None,) + wdown.shape[1:], (l, 0, 0)),
            _const_spec((1, d), (0, 0)),
        ],
        out_specs=pl.BlockSpec((tm, d), lambda i: (i, 0)),
        out_shape=jax.ShapeDtypeStruct((n_tok, d), f32),
        compiler_params=pltpu.CompilerParams(
            dimension_semantics=("arbitrary",),
            vmem_limit_bytes=VMEM_LIMIT_BYTES),
        name="ffn_final" if final_norm else "ffn",
    )(x2d, mod, nffn, wup, wdown, nfin)


def kernel(x, c, w_mod, b_mod, norm_mix, w_in, conv_w, v_norm, w_s, b_s,
           out_norm_a, out_norm_b, w_out, norm_ffn, w_up, w_down, norm_final):
    bsz, seq, d = x.shape
    depth = w_mod.shape[0]
    assert seq % TOKEN_TILE == 0 and ROW_GROUP % CHUNK == 0
    assert w_down.shape[1] % FF_CHUNK == 0

    mod = _modulation(c, w_mod, b_mod).reshape(depth, bsz, N_MOD, d)
    bias = jnp.repeat(jnp.swapaxes(b_s, 1, 2), HEAD_DIM, axis=2)
    rows = lambda p: p.reshape(depth, 1, -1)
    nmix, vnorm, ona, onb, nffn = map(rows, (norm_mix, v_norm, out_norm_a, out_norm_b, norm_ffn))
    win, wout, wup, wdown = (w.astype(bf16) for w in (w_in, w_out, w_up, w_down))

    xt = x.reshape(bsz * seq, d)
    for l in range(depth):
        xt = _mixer(xt, mod, nmix, win, conv_w, vnorm, w_s, bias, ona, onb, wout,
                    layer=l, seq=seq)
        xt = _ffn(xt, mod, nffn, wup, wdown, norm_final.reshape(1, d),
                  layer=l, seq=seq, final_norm=(l == depth - 1))
    return xt.reshape(bsz, seq, d)
```

```python
import functools

import jax
import jax.numpy as jnp
from jax import lax
from jax.experimental import pallas as pl
from jax.experimental.pallas import tpu as pltpu

EPS = 1e-6
HEAD_DIM = 64
CHUNK = 128
CONV_K = 3
N_MOD = 6
LANES = 128
SUBLANES = 8
TOKEN_TILE = 512
FF_CHUNK = 256
VMEM_LIMIT_BYTES = 60 * 1024 * 1024

f32 = jnp.float32
bf16 = jnp.bfloat16


def _rms(x):
    return x * lax.rsqrt(jnp.mean(x * x, axis=-1, keepdims=True) + EPS)


def _dot(a, b):
    return jnp.dot(a, b, preferred_element_type=f32)


def _const_spec(shape, index):
    return pl.BlockSpec(shape, lambda i: index, pipeline_mode=pl.Buffered(1))


def _mod_kernel(c_ref, w_ref, b_ref, o_ref):
    c = c_ref[...]
    o_ref[...] = _dot(c * jax.nn.sigmoid(c), w_ref[...]) + b_ref[...]


def _modulation(c, w_mod, b_mod):
    depth, d, n = w_mod.shape
    bsz = c.shape[0]
    nblk = n // d
    return pl.pallas_call(
        _mod_kernel,
        grid=(depth, nblk),
        in_specs=[
            pl.BlockSpec((bsz, d), lambda l, j: (0, 0)),
            pl.BlockSpec((None, d, d), lambda l, j: (l, 0, j)),
            pl.BlockSpec((None, 1, d), lambda l, j: (l, 0, j)),
        ],
        out_specs=pl.BlockSpec((None, bsz, d), lambda l, j: (l, 0, j)),
        out_shape=jax.ShapeDtypeStruct((depth, bsz, n), f32),
        compiler_params=pltpu.CompilerParams(
            dimension_semantics=("arbitrary", "arbitrary")),
        name="modulation",
    )(c, w_mod, b_mod.reshape(depth, 1, n))


def _layer_kernel(x_ref, modm_ref, modf_ref, nmix_ref, win_ref, convw_ref, vnorm_ref,
                  ws_ref, bias_ref, ona_ref, onb_ref, wout_ref, nffn_ref, wup_ref,
                  wdown_ref, nfin_ref, o_ref,
                  carry_ref, wsk_ref, x1_ref, h2n_ref, h2c_ref, *, tiles_per_seq, final_norm):
    i = pl.program_id(0)
    tm = x_ref.shape[0]
    cw = convw_ref.shape[1]
    sw = vnorm_ref.shape[1]
    d_ff = wdown_ref.shape[0]
    n_pairs = wsk_ref.shape[0]
    put = i % 2
    get = 1 - put

    @pl.when(i == 0)
    def _():
        r = lax.broadcasted_iota(jnp.int32, (CHUNK, CHUNK), 0)
        col = lax.broadcasted_iota(jnp.int32, (CHUNK, CHUNK), 1)
        tril = col <= r
        for p in range(n_pairs):
            a = jnp.where(tril, ws_ref[2 * p], 0.0)
            b = jnp.where(tril, ws_ref[2 * p + 1], 0.0)
            wsk_ref[p] = jnp.concatenate([a, b], axis=1).astype(bf16)
        x1_ref[1] = jnp.zeros((tm, x1_ref.shape[2]), f32)
        h2n_ref[...] = jnp.zeros_like(h2n_ref)

    @pl.when(i % tiles_per_seq == 0)
    def _():
        carry_ref[...] = jnp.zeros_like(carry_ref)

    h2c_ref[...] = h2n_ref[...]
    h2 = h2c_ref[...]
    ffn = {"acc": None}

    def ffn_chunk(c):
        lo = c * FF_CHUNK
        gt = _dot(h2, wup_ref[:, lo:lo + FF_CHUNK])
        up = _dot(h2, wup_ref[:, d_ff + lo:d_ff + lo + FF_CHUNK])
        a = (gt * jax.nn.sigmoid(gt) * up).astype(bf16)
        part = _dot(a, wdown_ref[lo:lo + FF_CHUNK, :])
        ffn["acc"] = part if ffn["acc"] is None else ffn["acc"] + part

    def ffn_finish():
        y = x1_ref[get] + modf_ref[5:6, :] * ffn["acc"]
        if final_norm:
            y = _rms(y) * nfin_ref[...]
        o_ref[...] = y

    mix = {}

    def mix_modulate():
        x = x_ref[...]
        mix["h"] = ((_rms(x) * nmix_ref[...]) * (1.0 + modm_ref[1:2, :])
                    + modm_ref[0:1, :]).astype(bf16)

    def mix_project_sgu():
        mix["uv"] = _dot(mix["h"], win_ref[:, 3 * cw:])

    def mix_project_conv():
        mix["bch"] = _dot(mix["h"], win_ref[:, :3 * cw])

    def mix_sgu_gates():
        uv = mix.pop("uv")
        mix["ug"] = jax.nn.gelu(uv[:, :sw])
        mix["vn"] = (_rms(jax.nn.gelu(uv[:, sw:])) * vnorm_ref[...]).astype(bf16)

    def mix_sgu_mixing():
        vn = mix.pop("vn")
        lo_mask = lax.broadcasted_iota(jnp.int32, (CHUNK, LANES), 1) < HEAD_DIM
        zero = jnp.zeros((CHUNK, LANES), bf16)
        bias = bias_ref[...]
        mixed_rows = []
        for c in range(tm // CHUNK):
            blocks = []
            for p in range(n_pairs):
                vb = vn[c * CHUNK:(c + 1) * CHUNK, p * LANES:(p + 1) * LANES]
                rhs = jnp.concatenate(
                    [jnp.where(lo_mask, vb, zero), jnp.where(lo_mask, zero, vb)], axis=0)
                blocks.append(_dot(wsk_ref[p], rhs))
            mixed_rows.append(jnp.concatenate(blocks, axis=1) + bias)
        mixed = jnp.concatenate(mixed_rows, axis=0)
        mix["y_b"] = (_rms(mix.pop("ug") * mixed) * onb_ref[...]).astype(bf16)

    def mix_conv():
        bch = mix.pop("bch")
        w0 = convw_ref[0:1, :]
        w1 = convw_ref[1:2, :]
        w2 = convw_ref[2:3, :]
        bg = bch[:, 0:cw]
        z = bch[:, cw:2 * cw] * bch[:, 2 * cw:3 * cw]
        prev = carry_ref[...]
        carry_ref[...] = z[tm - SUBLANES:, :]
        r1 = pltpu.roll(z, 1, 0)
        r2 = pltpu.roll(z, 2, 0)
        conv = w2 * z + w1 * r1 + w0 * r2
        row8 = lax.broadcasted_iota(jnp.int32, (SUBLANES, cw), 0)
        z1h = jnp.where(row8 < 1, pltpu.roll(prev, 1, 0), r1[:SUBLANES])
        z2h = jnp.where(row8 < 2, pltpu.roll(prev, 2, 0), r2[:SUBLANES])
        conv_head = w2 * z[:SUBLANES] + w1 * z1h + w0 * z2h
        conv = jnp.concatenate([conv_head, conv[SUBLANES:]], axis=0)
        mix["y_a"] = (_rms(bg * conv) * ona_ref[...]).astype(bf16)

    def mix_finish():
        y = _dot(mix.pop("y_b"), wout_ref[cw:, :]) + _dot(mix.pop("y_a"), wout_ref[0:cw, :])
        x1 = x_ref[...] + modm_ref[2:3, :] * y
        x1_ref[put] = x1
        h2n_ref[...] = ((_rms(x1) * nffn_ref[...]) * (1.0 + modm_ref[4:5, :])
                        + modm_ref[3:4, :]).astype(bf16)

    mix_stages = {0: mix_modulate, 1: mix_project_sgu, 2: mix_project_conv, 3: mix_sgu_gates,
                  5: mix_sgu_mixing, 6: mix_conv, 7: mix_finish}
    for c in range(d_ff // FF_CHUNK):
        ffn_chunk(c)
        if c in mix_stages:
            mix_stages[c]()
    ffn_finish()


def _layer(x2d, mod, nmix, win, convw, vnorm, ws, bias, ona, onb, wout, nffn, wup, wdown,
           nfin, *, layer, seq, final_norm):
    n_tok, d = x2d.shape
    tm = TOKEN_TILE
    n_tiles = n_tok // tm
    tiles_per_seq = seq // tm
    cw = convw.shape[-1]
    sw = vnorm.shape[-1]
    n_heads = ws.shape[1]
    l = layer
    mix_tile = lambda i: jnp.minimum(i, n_tiles - 1)
    ffn_tile = lambda i: jnp.maximum(i - 1, 0)
    return pl.pallas_call(
        functools.partial(_layer_kernel, tiles_per_seq=tiles_per_seq, final_norm=final_norm),
        grid=(n_tiles + 1,),
        in_specs=[
            pl.BlockSpec((tm, d), lambda i: (mix_tile(i), 0)),
            pl.BlockSpec((None, None, N_MOD, d), lambda i: (l, mix_tile(i) // tiles_per_seq, 0, 0)),
            pl.BlockSpec((None, None, N_MOD, d), lambda i: (l, ffn_tile(i) // tiles_per_seq, 0, 0)),
            _const_spec((None, 1, d), (l, 0, 0)),
            _const_spec((None,) + win.shape[1:], (l, 0, 0)),
            _const_spec((None, CONV_K, cw), (l, 0, 0)),
            _const_spec((None, 1, sw), (l, 0, 0)),
            _const_spec((None, n_heads, CHUNK, CHUNK), (l, 0, 0, 0)),
            _const_spec((None, CHUNK, sw), (l, 0, 0)),
            _const_spec((None, 1, cw), (l, 0, 0)),
            _const_spec((None, 1, sw), (l, 0, 0)),
            _const_spec((None,) + wout.shape[1:], (l, 0, 0)),
            _const_spec((None, 1, d), (l, 0, 0)),
            _const_spec((None,) + wup.shape[1:], (l, 0, 0)),
            _const_spec((None,) + wdown.shape[1:], (l, 0, 0)),
            _const_spec((1, d), (0, 0)),
        ],
        out_specs=pl.BlockSpec((tm, d), lambda i: (ffn_tile(i), 0)),
        out_shape=jax.ShapeDtypeStruct((n_tok, d), f32),
        scratch_shapes=[
            pltpu.VMEM((SUBLANES, cw), f32),
            pltpu.VMEM((n_heads // 2, CHUNK, 2 * CHUNK), bf16),
            pltpu.VMEM((2, tm, d), f32),
            pltpu.VMEM((tm, d), bf16),
            pltpu.VMEM((tm, d), bf16),
        ],
        compiler_params=pltpu.CompilerParams(
            dimension_semantics=("arbitrary",),
            vmem_limit_bytes=VMEM_LIMIT_BYTES),
        name="layer_final" if final_norm else "layer",
    )(x2d, mod, mod, nmix, win, convw, vnorm, ws, bias, ona, onb, wout, nffn, wup, wdown, nfin)


def kernel(x, c, w_mod, b_mod, norm_mix, w_in, conv_w, v_norm, w_s, b_s,
           out_norm_a, out_norm_b, w_out, norm_ffn, w_up, w_down, norm_final):
    bsz, seq, d = x.shape
    depth = w_mod.shape[0]
    assert seq % TOKEN_TILE == 0 and TOKEN_TILE % CHUNK == 0
    assert w_down.shape[1] % FF_CHUNK == 0

    mod = _modulation(c, w_mod, b_mod).reshape(depth, bsz, N_MOD, d)
    bias = jnp.repeat(jnp.swapaxes(b_s, 1, 2), HEAD_DIM, axis=2)
    rows = lambda p: p.reshape(depth, 1, -1)
    nmix, vnorm, ona, onb, nffn = map(rows, (norm_mix, v_norm, out_norm_a, out_norm_b, norm_ffn))
    win, wout, wup, wdown = (w.astype(bf16) for w in (w_in, w_out, w_up, w_down))

    xt = x.reshape(bsz * seq, d)
    for l in range(depth):
        xt = _layer(xt, mod, nmix, win, conv_w, vnorm, w_s, bias, ona, onb, wout, nffn,
                    wup, wdown, norm_final.reshape(1, d),
                    layer=l, seq=seq, final_norm=(l == depth - 1))
    return xt.reshape(bsz, seq, d)
```

```python
import functools

import jax
import jax.numpy as jnp
from jax import lax
from jax.experimental import pallas as pl
from jax.experimental.pallas import tpu as pltpu

EPS = 1e-6
HEAD_DIM = 64
CHUNK = 128
CONV_K = 3
N_MOD = 6
LANES = 128
SUBLANES = 8
BF16_TILE_ROWS = 16
ROW_GROUP = 512
GROUPS_PER_STEP = 2
ROW_BLOCK = 32
TOKEN_TILE = ROW_GROUP * GROUPS_PER_STEP
FF_CHUNK = 256
VMEM_LIMIT_BYTES = 56 * 1024 * 1024

f32 = jnp.float32
bf16 = jnp.bfloat16


def _rms(x):
    return x * lax.rsqrt(jnp.mean(x * x, axis=-1, keepdims=True) + EPS)


def _dot(a, b):
    return jnp.dot(a, b, preferred_element_type=f32)


def _const_spec(shape, index):
    return pl.BlockSpec(shape, lambda i: index, pipeline_mode=pl.Buffered(1))


def _cast_job_specs(jobs, n_steps):
    in_specs, out_specs, out_shapes = [], [], []
    for w, layer in jobs:
        _, r, c = w.shape
        span = next(k for k in (1, 2, 4, 8)
                    if n_steps % k == 0 and (r * k) % (n_steps * BF16_TILE_ROWS) == 0)
        rows = r * span // n_steps
        in_specs.append(pl.BlockSpec((None, rows, c), lambda i, l=layer, k=span: (l, i // k, 0)))
        out_specs.append(pl.BlockSpec((rows, c), lambda i, k=span: (i // k, 0)))
        out_shapes.append(jax.ShapeDtypeStruct((r, c), bf16))
    return in_specs, out_specs, out_shapes


def _run_cast_jobs(src_refs, dst_refs):
    for src, dst in zip(src_refs, dst_refs):
        dst[...] = src[...].astype(bf16)


def _mod_kernel(c_ref, w_ref, b_ref, o_ref):
    c = c_ref[...]
    o_ref[...] = _dot(c * jax.nn.sigmoid(c), w_ref[...]) + b_ref[...]


def _modulation(c, w_mod, b_mod):
    depth, d, n = w_mod.shape
    bsz = c.shape[0]
    nblk = n // d
    return pl.pallas_call(
        _mod_kernel,
        grid=(depth, nblk),
        in_specs=[
            pl.BlockSpec((bsz, d), lambda l, j: (0, 0)),
            pl.BlockSpec((None, d, d), lambda l, j: (l, 0, j)),
            pl.BlockSpec((None, 1, d), lambda l, j: (l, 0, j)),
        ],
        out_specs=pl.BlockSpec((None, bsz, d), lambda l, j: (l, 0, j)),
        out_shape=jax.ShapeDtypeStruct((depth, bsz, n), f32),
        compiler_params=pltpu.CompilerParams(
            dimension_semantics=("arbitrary", "arbitrary")),
        name="modulation",
    )(c, w_mod, b_mod.reshape(depth, 1, n))


def _mixer_kernel(x_ref, mod_ref, nmix_ref, win_ref, convw_ref, vnorm_ref,
                  ws_ref, bias_ref, ona_ref, onb_ref, wout_ref, *refs,
                  tiles_per_seq, n_cast):
    cast_src, o_ref, cast_dst = refs[:n_cast], refs[n_cast], refs[n_cast + 1:2 * n_cast + 1]
    wsk_ref, h_ref, uv_ref, bch_ref, z_ref, ug_ref, vn_ref, mixed_ref, y_ref = refs[2 * n_cast + 1:]
    _run_cast_jobs(cast_src, cast_dst)
    i = pl.program_id(0)
    cw = convw_ref.shape[1]
    sw = vnorm_ref.shape[1]
    n_pairs = wsk_ref.shape[0]
    rg = ROW_GROUP
    rb = ROW_BLOCK
    last = GROUPS_PER_STEP - 1

    @pl.when(i == 0)
    def _():
        r = lax.broadcasted_iota(jnp.int32, (CHUNK, CHUNK), 0)
        col = lax.broadcasted_iota(jnp.int32, (CHUNK, CHUNK), 1)
        tril = col <= r
        for p in range(n_pairs):
            a = jnp.where(tril, ws_ref[2 * p], 0.0)
            b = jnp.where(tril, ws_ref[2 * p + 1], 0.0)
            wsk_ref[p] = jnp.concatenate([a, b], axis=1).astype(bf16)

    @pl.when(i % tiles_per_seq == 0)
    def _():
        z_ref[0, 0:SUBLANES, :] = jnp.zeros((SUBLANES, cw), f32)

    @pl.when(i % tiles_per_seq != 0)
    def _():
        z_ref[0, 0:SUBLANES, :] = z_ref[last, rg:rg + SUBLANES, :]

    sh = mod_ref[0:1, :]
    sc = mod_ref[1:2, :]
    gate = mod_ref[2:3, :]
    w0 = convw_ref[0:1, :]
    w1 = convw_ref[1:2, :]
    w2 = convw_ref[2:3, :]
    lo_mask = lax.broadcasted_iota(jnp.int32, (CHUNK, LANES), 1) < HEAD_DIM
    zero = jnp.zeros((CHUNK, LANES), bf16)
    blocks = [(r0, r0 + rb) for r0 in range(0, rg, rb)]

    def modulate(g):
        for r0, r1 in blocks:
            x = x_ref[g * rg + r0:g * rg + r1, :]
            h_ref[g, r0:r1, :] = ((_rms(x) * nmix_ref[...]) * (1.0 + sc) + sh).astype(bf16)

    def project_sgu(g):
        uv_ref[g] = _dot(h_ref[g], win_ref[:, 3 * cw:])

    def project_conv(g):
        bch_ref[g] = _dot(h_ref[g], win_ref[:, :3 * cw])

    def sgu_gates(g):
        for r0, r1 in blocks:
            ug_ref[g, r0:r1, :] = jax.nn.gelu(uv_ref[g, r0:r1, :sw])
            v = jax.nn.gelu(uv_ref[g, r0:r1, sw:])
            vn_ref[g, r0:r1, :] = (_rms(v) * vnorm_ref[...]).astype(bf16)

    def sgu_mixing(g):
        for c in range(rg // CHUNK):
            for p in range(n_pairs):
                vb = vn_ref[g, c * CHUNK:(c + 1) * CHUNK, p * LANES:(p + 1) * LANES]
                rhs = jnp.concatenate(
                    [jnp.where(lo_mask, vb, zero), jnp.where(lo_mask, zero, vb)], axis=0)
                mixed_ref[g, c * CHUNK:(c + 1) * CHUNK, p * LANES:(p + 1) * LANES] = (
                    _dot(wsk_ref[p], rhs))

    def sgu_out(g):
        for r0, r1 in blocks:
            b0 = r0 % CHUNK
            mixed = mixed_ref[g, r0:r1, :] + bias_ref[b0:b0 + rb, :]
            y_b = _rms(ug_ref[g, r0:r1, :] * mixed) * onb_ref[...]
            y_ref[g, r0:r1, cw:] = y_b.astype(bf16)

    def conv_gate(g):
        if g > 0:
            z_ref[g, 0:SUBLANES, :] = z_ref[g - 1, rg:rg + SUBLANES, :]
        for r0, r1 in blocks:
            z_ref[g, SUBLANES + r0:SUBLANES + r1, :] = (
                bch_ref[g, r0:r1, cw:2 * cw] * bch_ref[g, r0:r1, 2 * cw:3 * cw])

    def conv_out(g):
        for r0, r1 in blocks:
            zext = z_ref[g, r0:SUBLANES + r1, :]
            z = zext[SUBLANES:]
            z1 = pltpu.roll(zext, 1, 0)[SUBLANES:]
            z2 = pltpu.roll(zext, 2, 0)[SUBLANES:]
            conv = w2 * z + w1 * z1 + w0 * z2
            y_a = _rms(bch_ref[g, r0:r1, 0:cw] * conv) * ona_ref[...]
            y_ref[g, r0:r1, :cw] = y_a.astype(bf16)

    def finish(g):
        y = _dot(y_ref[g], wout_ref[...])
        o_ref[g * rg:(g + 1) * rg, :] = x_ref[g * rg:(g + 1) * rg, :] + gate * y

    assert GROUPS_PER_STEP == 2
    for stage, g in ((modulate, 0), (project_sgu, 0), (modulate, 1), (project_conv, 0),
                     (sgu_gates, 0), (project_sgu, 1), (conv_gate, 0), (conv_out, 0),
                     (sgu_mixing, 0), (project_conv, 1), (sgu_out, 0), (sgu_gates, 1),
                     (finish, 0), (conv_gate, 1), (conv_out, 1), (sgu_mixing, 1),
                     (sgu_out, 1), (finish, 1)):
        stage(g)


def _mixer(x2d, mod, nmix, win, convw, vnorm, ws, bias, ona, onb, wout, *, layer, seq,
           cast_jobs=()):
    n_tok, d = x2d.shape
    tm = TOKEN_TILE
    tiles_per_seq = seq // tm
    cast_in, cast_out, cast_shapes = _cast_job_specs(cast_jobs, n_tok // tm)
    cw = convw.shape[-1]
    sw = vnorm.shape[-1]
    n_heads = ws.shape[1]
    l = layer
    ng, rg = GROUPS_PER_STEP, ROW_GROUP
    return pl.pallas_call(
        functools.partial(_mixer_kernel, tiles_per_seq=tiles_per_seq, n_cast=len(cast_jobs)),
        grid=(n_tok // tm,),
        in_specs=[
            pl.BlockSpec((tm, d), lambda i: (i, 0)),
            pl.BlockSpec((None, None, N_MOD, d), lambda i: (l, i // tiles_per_seq, 0, 0)),
            _const_spec((None, 1, d), (l, 0, 0)),
            _const_spec(win.shape, (0, 0)),
            _const_spec((None, CONV_K, cw), (l, 0, 0)),
            _const_spec((None, 1, sw), (l, 0, 0)),
            _const_spec((None, n_heads, CHUNK, CHUNK), (l, 0, 0, 0)),
            _const_spec((None, CHUNK, sw), (l, 0, 0)),
            _const_spec((None, 1, cw), (l, 0, 0)),
            _const_spec((None, 1, sw), (l, 0, 0)),
            _const_spec(wout.shape, (0, 0)),
        ] + cast_in,
        out_specs=[pl.BlockSpec((tm, d), lambda i: (i, 0))] + cast_out,
        out_shape=[jax.ShapeDtypeStruct((n_tok, d), f32)] + cast_shapes,
        scratch_shapes=[
            pltpu.VMEM((n_heads // 2, CHUNK, 2 * CHUNK), bf16),
            pltpu.VMEM((ng, rg, d), bf16),
            pltpu.VMEM((ng, rg, 2 * sw), f32),
            pltpu.VMEM((ng, rg, 3 * cw), f32),
            pltpu.VMEM((ng, rg + SUBLANES, cw), f32),
            pltpu.VMEM((ng, rg, sw), f32),
            pltpu.VMEM((ng, rg, sw), bf16),
            pltpu.VMEM((ng, rg, sw), f32),
            pltpu.VMEM((ng, rg, cw + sw), bf16),
        ],
        compiler_params=pltpu.CompilerParams(
            dimension_semantics=("arbitrary",),
            vmem_limit_bytes=VMEM_LIMIT_BYTES),
        name="mixer",
    )(x2d, mod, nmix, win, convw, vnorm, ws, bias, ona, onb, wout, *[w for w, _ in cast_jobs])


def _ffn_kernel(x_ref, mod_ref, nffn_ref, wup_ref, wdown_ref, nfin_ref, *refs,
                final_norm, n_cast):
    cast_src, o_ref, cast_dst = refs[:n_cast], refs[n_cast], refs[n_cast + 1:]
    _run_cast_jobs(cast_src, cast_dst)
    d_ff = wdown_ref.shape[0]
    rg = ROW_GROUP
    groups = range(GROUPS_PER_STEP)
    sh = mod_ref[3:4, :]
    sc = mod_ref[4:5, :]
    gate = mod_ref[5:6, :]
    xs = [x_ref[g * rg:(g + 1) * rg, :] for g in groups]
    hs = [((_rms(x) * nffn_ref[...]) * (1.0 + sc) + sh).astype(bf16) for x in xs]
    accs = [None for _ in groups]
    for c in range(d_ff // FF_CHUNK):
        lo = c * FF_CHUNK
        for g in groups:
            gt = _dot(hs[g], wup_ref[:, lo:lo + FF_CHUNK])
            up = _dot(hs[g], wup_ref[:, d_ff + lo:d_ff + lo + FF_CHUNK])
            a = (gt * jax.nn.sigmoid(gt) * up).astype(bf16)
            part = _dot(a, wdown_ref[lo:lo + FF_CHUNK, :])
            accs[g] = part if accs[g] is None else accs[g] + part
    for g in groups:
        y = xs[g] + gate * accs[g]
        if final_norm:
            y = _rms(y) * nfin_ref[...]
        o_ref[g * rg:(g + 1) * rg, :] = y


def _ffn(x2d, mod, nffn, wup, wdown, nfin, *, layer, seq, final_norm, cast_jobs=()):
    n_tok, d = x2d.shape
    tm = TOKEN_TILE
    tiles_per_seq = seq // tm
    l = layer
    cast_in, cast_out, cast_shapes = _cast_job_specs(cast_jobs, n_tok // tm)
    return pl.pallas_call(
        functools.partial(_ffn_kernel, final_norm=final_norm, n_cast=len(cast_jobs)),
        grid=(n_tok // tm,),
        in_specs=[
            pl.BlockSpec((tm, d), lambda i: (i, 0)),
            pl.BlockSpec((None, None, N_MOD, d), lambda i: (l, i // tiles_per_seq, 0, 0)),
            _const_spec((None, 1, d), (l, 0, 0)),
            _const_spec(wup.shape, (0, 0)),
            _const_spec(wdown.shape, (0, 0)),
            _const_spec((1, d), (0, 0)),
        ] + cast_in,
        out_specs=[pl.BlockSpec((tm, d), lambda i: (i, 0))] + cast_out,
        out_shape=[jax.ShapeDtypeStruct((n_tok, d), f32)] + cast_shapes,
        compiler_params=pltpu.CompilerParams(
            dimension_semantics=("arbitrary",),
            vmem_limit_bytes=VMEM_LIMIT_BYTES),
        name="ffn_final" if final_norm else "ffn",
    )(x2d, mod, nffn, wup, wdown, nfin, *[w for w, _ in cast_jobs])


def kernel(x, c, w_mod, b_mod, norm_mix, w_in, conv_w, v_norm, w_s, b_s,
           out_norm_a, out_norm_b, w_out, norm_ffn, w_up, w_down, norm_final):
    bsz, seq, d = x.shape
    depth = w_mod.shape[0]
    assert seq % TOKEN_TILE == 0 and ROW_GROUP % CHUNK == 0
    assert w_down.shape[1] % FF_CHUNK == 0

    mod = _modulation(c, w_mod, b_mod).reshape(depth, bsz, N_MOD, d)
    bias = jnp.repeat(jnp.swapaxes(b_s, 1, 2), HEAD_DIM, axis=2)
    rows = lambda p: p.reshape(depth, 1, -1)
    nmix, vnorm, ona, onb, nffn = map(rows, (norm_mix, v_norm, out_norm_a, out_norm_b, norm_ffn))
    win, wout = w_in[0].astype(bf16), w_out[0].astype(bf16)

    xt = x.reshape(bsz * seq, d)
    for l in range(depth):
        xt, wup, wdown = _mixer(xt, mod, nmix, win, conv_w, vnorm, w_s, bias, ona, onb, wout,
                                layer=l, seq=seq, cast_jobs=((w_up, l), (w_down, l)))
        nxt = ((w_in, l + 1), (w_out, l + 1)) if l + 1 < depth else ()
        xt, *casts = _ffn(xt, mod, nffn, wup, wdown, norm_final.reshape(1, d),
                          layer=l, seq=seq, final_norm=(l == depth - 1), cast_jobs=nxt)
        if casts:
            win, wout = casts
    return xt.reshape(bsz, seq, d)
```

```python
import functools

import jax
import jax.numpy as jnp
from jax import lax
from jax.experimental import pallas as pl
from jax.experimental.pallas import tpu as pltpu

EPS = 1e-6
HEAD_DIM = 64
CHUNK = 128
CONV_K = 3
N_MOD = 6
LANES = 128
SUBLANES = 8
BF16_TILE_ROWS = 16
ROW_GROUP = 512
GROUPS_PER_STEP = 2
TOKEN_TILE = ROW_GROUP * GROUPS_PER_STEP
FF_CHUNK = 256
VMEM_LIMIT_BYTES = 56 * 1024 * 1024

f32 = jnp.float32
bf16 = jnp.bfloat16


def _rms(x):
    return x * lax.rsqrt(jnp.mean(x * x, axis=-1, keepdims=True) + EPS)


def _dot(a, b):
    return jnp.dot(a, b, preferred_element_type=f32)


def _const_spec(shape, index):
    return pl.BlockSpec(shape, lambda i: index, pipeline_mode=pl.Buffered(1))


def _cast_job_specs(jobs, n_steps):
    in_specs, out_specs, out_shapes = [], [], []
    for w, layer in jobs:
        _, r, c = w.shape
        span = next(k for k in (1, 2, 4, 8)
                    if n_steps % k == 0 and (r * k) % (n_steps * BF16_TILE_ROWS) == 0)
        rows = r * span // n_steps
        in_specs.append(pl.BlockSpec((None, rows, c), lambda i, l=layer, k=span: (l, i // k, 0)))
        out_specs.append(pl.BlockSpec((rows, c), lambda i, k=span: (i // k, 0)))
        out_shapes.append(jax.ShapeDtypeStruct((r, c), bf16))
    return in_specs, out_specs, out_shapes


def _run_cast_jobs(src_refs, dst_refs):
    for src, dst in zip(src_refs, dst_refs):
        dst[...] = src[...].astype(bf16)


def _mod_kernel(c_ref, w_ref, b_ref, o_ref):
    c = c_ref[...]
    o_ref[...] = _dot(c * jax.nn.sigmoid(c), w_ref[...]) + b_ref[...]


def _modulation(c, w_mod, b_mod):
    depth, d, n = w_mod.shape
    bsz = c.shape[0]
    nblk = n // d
    return pl.pallas_call(
        _mod_kernel,
        grid=(depth, nblk),
        in_specs=[
            pl.BlockSpec((bsz, d), lambda l, j: (0, 0)),
            pl.BlockSpec((None, d, d), lambda l, j: (l, 0, j)),
            pl.BlockSpec((None, 1, d), lambda l, j: (l, 0, j)),
        ],
        out_specs=pl.BlockSpec((None, bsz, d), lambda l, j: (l, 0, j)),
        out_shape=jax.ShapeDtypeStruct((depth, bsz, n), f32),
        compiler_params=pltpu.CompilerParams(
            dimension_semantics=("arbitrary", "arbitrary")),
        name="modulation",
    )(c, w_mod, b_mod.reshape(depth, 1, n))


def _mixer_kernel(x_ref, mod_ref, nmix_ref, win_ref, convw_ref, vnorm_ref,
                  ws_ref, bias_ref, ona_ref, onb_ref, wout_ref, *refs,
                  tiles_per_seq, n_cast):
    cast_src, o_ref, cast_dst = refs[:n_cast], refs[n_cast], refs[n_cast + 1:2 * n_cast + 1]
    carry_ref, wsk_ref = refs[2 * n_cast + 1:]
    _run_cast_jobs(cast_src, cast_dst)
    i = pl.program_id(0)
    cw = convw_ref.shape[1]
    sw = vnorm_ref.shape[1]
    n_pairs = wsk_ref.shape[0]
    rg = ROW_GROUP

    @pl.when(i == 0)
    def _():
        r = lax.broadcasted_iota(jnp.int32, (CHUNK, CHUNK), 0)
        col = lax.broadcasted_iota(jnp.int32, (CHUNK, CHUNK), 1)
        tril = col <= r
        for p in range(n_pairs):
            a = jnp.where(tril, ws_ref[2 * p], 0.0)
            b = jnp.where(tril, ws_ref[2 * p + 1], 0.0)
            wsk_ref[p] = jnp.concatenate([a, b], axis=1).astype(bf16)

    @pl.when(i % tiles_per_seq == 0)
    def _():
        carry_ref[...] = jnp.zeros_like(carry_ref)

    sh = mod_ref[0:1, :]
    scale = nmix_ref[...] * (1.0 + mod_ref[1:2, :])
    gate = mod_ref[2:3, :]
    w0 = convw_ref[0:1, :]
    w1 = convw_ref[1:2, :]
    w2 = convw_ref[2:3, :]
    row8 = lax.broadcasted_iota(jnp.int32, (SUBLANES, cw), 0)
    lo_mask = lax.broadcasted_iota(jnp.int32, (CHUNK, LANES), 1) < HEAD_DIM
    zero = jnp.zeros((CHUNK, LANES), bf16)

    def project(x):
        h = _rms(x) * scale + sh
        return _dot(h.astype(bf16), win_ref[...])

    def conv_branch(proj, prev):
        bg = proj[:, 0:cw]
        z = proj[:, cw:2 * cw] * proj[:, 2 * cw:3 * cw]
        r1 = pltpu.roll(z, 1, 0)
        r2 = pltpu.roll(z, 2, 0)
        conv = w2 * z + w1 * r1 + w0 * r2
        z1h = jnp.where(row8 < 1, pltpu.roll(prev, 1, 0), r1[:SUBLANES])
        z2h = jnp.where(row8 < 2, pltpu.roll(prev, 2, 0), r2[:SUBLANES])
        conv_head = w2 * z[:SUBLANES] + w1 * z1h + w0 * z2h
        conv = jnp.concatenate([conv_head, conv[SUBLANES:]], axis=0)
        y_a = _rms(bg * conv) * ona_ref[...]
        return y_a.astype(bf16), z[rg - SUBLANES:, :]

    def sgu_branch(proj):
        ug = jax.nn.gelu(proj[:, 3 * cw:3 * cw + sw])
        vn = (_rms(jax.nn.gelu(proj[:, 3 * cw + sw:])) * vnorm_ref[...]).astype(bf16)
        bias = bias_ref[...]
        mixed_rows = []
        for c in range(rg // CHUNK):
            blocks = []
            for p in range(n_pairs):
                vb = vn[c * CHUNK:(c + 1) * CHUNK, p * LANES:(p + 1) * LANES]
                rhs = jnp.concatenate(
                    [jnp.where(lo_mask, vb, zero), jnp.where(lo_mask, zero, vb)], axis=0)
                blocks.append(_dot(wsk_ref[p], rhs))
            mixed_rows.append(jnp.concatenate(blocks, axis=1) + bias)
        mixed = jnp.concatenate(mixed_rows, axis=0)
        return (_rms(ug * mixed) * onb_ref[...]).astype(bf16)

    prev = carry_ref[...]
    xs = [x_ref[g * rg:(g + 1) * rg, :] for g in range(GROUPS_PER_STEP)]
    projs = [project(x) for x in xs]
    for g in range(GROUPS_PER_STEP):
        y_b = sgu_branch(projs[g])
        y_a, prev = conv_branch(projs[g], prev)
        y = _dot(y_b, wout_ref[cw:, :]) + _dot(y_a, wout_ref[0:cw, :])
        o_ref[g * rg:(g + 1) * rg, :] = xs[g] + gate * y
    carry_ref[...] = prev


def _mixer(x2d, mod, nmix, win, convw, vnorm, ws, bias, ona, onb, wout, *, layer, seq,
           cast_jobs=()):
    n_tok, d = x2d.shape
    tm = TOKEN_TILE
    tiles_per_seq = seq // tm
    cast_in, cast_out, cast_shapes = _cast_job_specs(cast_jobs, n_tok // tm)
    cw = convw.shape[-1]
    sw = vnorm.shape[-1]
    n_heads = ws.shape[1]
    l = layer
    return pl.pallas_call(
        functools.partial(_mixer_kernel, tiles_per_seq=tiles_per_seq, n_cast=len(cast_jobs)),
        grid=(n_tok // tm,),
        in_specs=[
            pl.BlockSpec((tm, d), lambda i: (i, 0)),
            pl.BlockSpec((None, None, N_MOD, d), lambda i: (l, i // tiles_per_seq, 0, 0)),
            _const_spec((None, 1, d), (l, 0, 0)),
            _const_spec(win.shape, (0, 0)),
            _const_spec((None, CONV_K, cw), (l, 0, 0)),
            _const_spec((None, 1, sw), (l, 0, 0)),
            _const_spec((None, n_heads, CHUNK, CHUNK), (l, 0, 0, 0)),
            _const_spec((None, CHUNK, sw), (l, 0, 0)),
            _const_spec((None, 1, cw), (l, 0, 0)),
            _const_spec((None, 1, sw), (l, 0, 0)),
            _const_spec(wout.shape, (0, 0)),
        ] + cast_in,
        out_specs=[pl.BlockSpec((tm, d), lambda i: (i, 0))] + cast_out,
        out_shape=[jax.ShapeDtypeStruct((n_tok, d), f32)] + cast_shapes,
        scratch_shapes=[
            pltpu.VMEM((SUBLANES, cw), f32),
            pltpu.VMEM((n_heads // 2, CHUNK, 2 * CHUNK), bf16),
        ],
        compiler_params=pltpu.CompilerParams(
            dimension_semantics=("arbitrary",),
            vmem_limit_bytes=VMEM_LIMIT_BYTES),
        name="mixer",
    )(x2d, mod, nmix, win, convw, vnorm, ws, bias, ona, onb, wout, *[w for w, _ in cast_jobs])


def _ffn_kernel(x_ref, mod_ref, nffn_ref, wup_ref, wdown_ref, nfin_ref, *refs,
                final_norm, n_cast):
    cast_src, o_ref, cast_dst = refs[:n_cast], refs[n_cast], refs[n_cast + 1:]
    _run_cast_jobs(cast_src, cast_dst)
    d_ff = wdown_ref.shape[0]
    rg = ROW_GROUP
    groups = range(GROUPS_PER_STEP)
    sh = mod_ref[3:4, :]
    scale = nffn_ref[...] * (1.0 + mod_ref[4:5, :])
    gate = mod_ref[5:6, :]
    xs = [x_ref[g * rg:(g + 1) * rg, :] for g in groups]
    hs = [(_rms(x) * scale + sh).astype(bf16) for x in xs]
    accs = [None for _ in groups]
    for c in range(d_ff // FF_CHUNK):
        lo = c * FF_CHUNK
        for g in groups:
            gt = _dot(hs[g], wup_ref[:, lo:lo + FF_CHUNK])
            up = _dot(hs[g], wup_ref[:, d_ff + lo:d_ff + lo + FF_CHUNK])
            a = (gt * jax.nn.sigmoid(gt) * up).astype(bf16)
            part = _dot(a, wdown_ref[lo:lo + FF_CHUNK, :])
            accs[g] = part if accs[g] is None else accs[g] + part
    for g in groups:
        y = xs[g] + gate * accs[g]
        if final_norm:
            y = _rms(y) * nfin_ref[...]
        o_ref[g * rg:(g + 1) * rg, :] = y


def _ffn(x2d, mod, nffn, wup, wdown, nfin, *, layer, seq, final_norm, cast_jobs=()):
    n_tok, d = x2d.shape
    tm = TOKEN_TILE
    tiles_per_seq = seq // tm
    l = layer
    cast_in, cast_out, cast_shapes = _cast_job_specs(cast_jobs, n_tok // tm)
    return pl.pallas_call(
        functools.partial(_ffn_kernel, final_norm=final_norm, n_cast=len(cast_jobs)),
        grid=(n_tok // tm,),
        in_specs=[
            pl.BlockSpec((tm, d), lambda i: (i, 0)),
            pl.BlockSpec((None, None, N_MOD, d), lambda i: (l, i // tiles_per_seq, 0, 0)),
            _const_spec((None, 1, d), (l, 0, 0)),
            _const_spec(wup.shape, (0, 0)),
            _const_spec(wdown.shape, (0, 0)),
            _const_spec((1, d), (0, 0)),
        ] + cast_in,
        out_specs=[pl.BlockSpec((tm, d), lambda i: (i, 0))] + cast_out,
        out_shape=[jax.ShapeDtypeStruct((n_tok, d), f32)] + cast_shapes,
        compiler_params=pltpu.CompilerParams(
            dimension_semantics=("arbitrary",),
            vmem_limit_bytes=VMEM_LIMIT_BYTES),
        name="ffn_final" if final_norm else "ffn",
    )(x2d, mod, nffn, wup, wdown, nfin, *[w for w, _ in cast_jobs])


def kernel(x, c, w_mod, b_mod, norm_mix, w_in, conv_w, v_norm, w_s, b_s,
           out_norm_a, out_norm_b, w_out, norm_ffn, w_up, w_down, norm_final):
    bsz, seq, d = x.shape
    depth = w_mod.shape[0]
    assert seq % TOKEN_TILE == 0 and ROW_GROUP % CHUNK == 0
    assert w_down.shape[1] % FF_CHUNK == 0

    mod = _modulation(c, w_mod, b_mod).reshape(depth, bsz, N_MOD, d)
    bias = jnp.repeat(jnp.swapaxes(b_s, 1, 2), HEAD_DIM, axis=2)
    rows = lambda p: p.reshape(depth, 1, -1)
    nmix, vnorm, ona, onb, nffn = map(rows, (norm_mix, v_norm, out_norm_a, out_norm_b, norm_ffn))
    win, wout = w_in[0].astype(bf16), w_out[0].astype(bf16)

    xt = x.reshape(bsz * seq, d)
    for l in range(depth):
        xt, wup, wdown = _mixer(xt, mod, nmix, win, conv_w, vnorm, w_s, bias, ona, onb, wout,
                                layer=l, seq=seq, cast_jobs=((w_up, l), (w_down, l)))
        nxt = ((w_in, l + 1), (w_out, l + 1)) if l + 1 < depth else ()
        xt, *casts = _ffn(xt, mod, nffn, wup, wdown, norm_final.reshape(1, d),
                          layer=l, seq=seq, final_norm=(l == depth - 1), cast_jobs=nxt)
        if casts:
            win, wout = casts
    return xt.reshape(bsz, seq, d)
```

```python
import functools

import jax
import jax.numpy as jnp
from jax import lax
from jax.experimental import pallas as pl
from jax.experimental.pallas import tpu as pltpu

EPS = 1e-6
GELU_K = 0.7978845608028654
GELU_C = 0.044715
HEAD_DIM = 64
CHUNK = 128
CONV_K = 3
N_MOD = 6
LANES = 128
SUBLANES = 8
BF16_TILE_ROWS = 16
ROW_GROUP = 512
GROUPS_PER_STEP = 2
TOKEN_TILE = ROW_GROUP * GROUPS_PER_STEP
FF_CHUNK = 256
VMEM_LIMIT_BYTES = 56 * 1024 * 1024

f32 = jnp.float32
bf16 = jnp.bfloat16


def _rms(x):
    return x * lax.rsqrt(jnp.mean(x * x, axis=-1, keepdims=True) + EPS)


def _dot(a, b):
    return jnp.dot(a, b, preferred_element_type=f32)


def _gelu(x):
    inner = x * (GELU_K + (GELU_K * GELU_C) * (x * x))
    half = 0.5 * x
    return half + half * jnp.tanh(inner)


def _const_spec(shape, index):
    return pl.BlockSpec(shape, lambda i: index, pipeline_mode=pl.Buffered(1))


def _cast_job_specs(jobs, n_steps):
    in_specs, out_specs, out_shapes = [], [], []
    for w, layer in jobs:
        _, r, c = w.shape
        span = next(k for k in (1, 2, 4, 8)
                    if n_steps % k == 0 and (r * k) % (n_steps * BF16_TILE_ROWS) == 0)
        rows = r * span // n_steps
        in_specs.append(pl.BlockSpec((None, rows, c), lambda i, l=layer, k=span: (l, i // k, 0)))
        out_specs.append(pl.BlockSpec((rows, c), lambda i, k=span: (i // k, 0)))
        out_shapes.append(jax.ShapeDtypeStruct((r, c), bf16))
    return in_specs, out_specs, out_shapes


def _run_cast_jobs(src_refs, dst_refs):
    for src, dst in zip(src_refs, dst_refs):
        dst[...] = src[...].astype(bf16)


def _mod_kernel(c_ref, w_ref, b_ref, o_ref):
    c = c_ref[...]
    bias = b_ref[pl.ds(pl.program_id(0), 1), :]
    o_ref[...] = _dot(c * jax.nn.sigmoid(c), w_ref[...]) + bias


def _modulation(c, w_mod, b_mod):
    depth, d, n = w_mod.shape
    bsz = c.shape[0]
    nblk = n // d
    return pl.pallas_call(
        _mod_kernel,
        grid=(depth, nblk),
        in_specs=[
            pl.BlockSpec((bsz, d), lambda l, j: (0, 0)),
            pl.BlockSpec((None, d, d), lambda l, j: (l, 0, j)),
            pl.BlockSpec((depth, d), lambda l, j: (0, j)),
        ],
        out_specs=pl.BlockSpec((None, bsz, d), lambda l, j: (l, 0, j)),
        out_shape=jax.ShapeDtypeStruct((depth, bsz, n), f32),
        compiler_params=pltpu.CompilerParams(
            dimension_semantics=("arbitrary", "arbitrary")),
        name="modulation",
    )(c, w_mod, b_mod)


def _mixer_kernel(x_ref, mod_ref, nmix_ref, win_ref, convw_ref, vnorm_ref,
                  ws_ref, bias_ref, ona_ref, onb_ref, wout_ref, *refs,
                  layer, tiles_per_seq, n_cast):
    cast_src, o_ref, cast_dst = refs[:n_cast], refs[n_cast], refs[n_cast + 1:2 * n_cast + 1]
    carry_ref, wsk_ref = refs[2 * n_cast + 1:]
    _run_cast_jobs(cast_src, cast_dst)
    i = pl.program_id(0)
    cw = convw_ref.shape[1]
    sw = vnorm_ref.shape[1]
    n_pairs = wsk_ref.shape[0]
    rg = ROW_GROUP

    @pl.when(i == 0)
    def _():
        r = lax.broadcasted_iota(jnp.int32, (CHUNK, CHUNK), 0)
        col = lax.broadcasted_iota(jnp.int32, (CHUNK, CHUNK), 1)
        tril = col <= r
        for p in range(n_pairs):
            a = jnp.where(tril, ws_ref[2 * p], 0.0)
            b = jnp.where(tril, ws_ref[2 * p + 1], 0.0)
            wsk_ref[p] = jnp.concatenate([a, b], axis=1).astype(bf16)

    @pl.when(i % tiles_per_seq == 0)
    def _():
        carry_ref[...] = jnp.zeros_like(carry_ref)

    vnorm = vnorm_ref[layer:layer + 1, :]
    ona = ona_ref[layer:layer + 1, :]
    onb = onb_ref[layer:layer + 1, :]
    sh = mod_ref[0:1, :]
    scale = nmix_ref[layer:layer + 1, :] * (1.0 + mod_ref[1:2, :])
    gate = mod_ref[2:3, :]
    w0 = convw_ref[0:1, :]
    w1 = convw_ref[1:2, :]
    w2 = convw_ref[2:3, :]
    row8 = lax.broadcasted_iota(jnp.int32, (SUBLANES, cw), 0)
    lo_mask = lax.broadcasted_iota(jnp.int32, (CHUNK, LANES), 1) < HEAD_DIM
    zero = jnp.zeros((CHUNK, LANES), bf16)

    def project(x):
        h = _rms(x) * scale + sh
        return _dot(h.astype(bf16), win_ref[...])

    def conv_branch(proj, prev):
        bg = proj[:, 0:cw]
        z = proj[:, cw:2 * cw] * proj[:, 2 * cw:3 * cw]
        r1 = pltpu.roll(z, 1, 0)
        r2 = pltpu.roll(z, 2, 0)
        conv = w2 * z + w1 * r1 + w0 * r2
        z1h = jnp.where(row8 < 1, pltpu.roll(prev, 1, 0), r1[:SUBLANES])
        z2h = jnp.where(row8 < 2, pltpu.roll(prev, 2, 0), r2[:SUBLANES])
        conv_head = w2 * z[:SUBLANES] + w1 * z1h + w0 * z2h
        conv = jnp.concatenate([conv_head, conv[SUBLANES:]], axis=0)
        y_a = _rms(bg * conv) * ona
        return y_a.astype(bf16), z[rg - SUBLANES:, :]

    def sgu_branch(proj):
        ug = _gelu(proj[:, 3 * cw:3 * cw + sw])
        vn = (_rms(_gelu(proj[:, 3 * cw + sw:])) * vnorm).astype(bf16)
        bias = bias_ref[...]
        mixed_rows = []
        for c in range(rg // CHUNK):
            blocks = []
            for p in range(n_pairs):
                vb = vn[c * CHUNK:(c + 1) * CHUNK, p * LANES:(p + 1) * LANES]
                rhs = jnp.concatenate(
                    [jnp.where(lo_mask, vb, zero), jnp.where(lo_mask, zero, vb)], axis=0)
                blocks.append(_dot(wsk_ref[p], rhs))
            mixed_rows.append(jnp.concatenate(blocks, axis=1) + bias)
        mixed = jnp.concatenate(mixed_rows, axis=0)
        return (_rms(ug * mixed) * onb).astype(bf16)

    prev = carry_ref[...]
    xs = [x_ref[g * rg:(g + 1) * rg, :] for g in range(GROUPS_PER_STEP)]
    projs = [project(x) for x in xs]
    for g in range(GROUPS_PER_STEP):
        y_b = sgu_branch(projs[g])
        y_a, prev = conv_branch(projs[g], prev)
        y = _dot(jnp.concatenate([y_a, y_b], axis=1), wout_ref[...])
        o_ref[g * rg:(g + 1) * rg, :] = xs[g] + gate * y
    carry_ref[...] = prev


def _mixer(x2d, mod, nmix, win, convw, vnorm, ws, bias, ona, onb, wout, *, layer, seq,
           cast_jobs=()):
    n_tok, d = x2d.shape
    tm = TOKEN_TILE
    tiles_per_seq = seq // tm
    cast_in, cast_out, cast_shapes = _cast_job_specs(cast_jobs, n_tok // tm)
    cw = convw.shape[-1]
    sw = vnorm.shape[-1]
    n_heads = ws.shape[1]
    l = layer
    return pl.pallas_call(
        functools.partial(_mixer_kernel, layer=l, tiles_per_seq=tiles_per_seq,
                          n_cast=len(cast_jobs)),
        grid=(n_tok // tm,),
        in_specs=[
            pl.BlockSpec((tm, d), lambda i: (i, 0)),
            pl.BlockSpec((None, None, N_MOD, d), lambda i: (l, i // tiles_per_seq, 0, 0)),
            _const_spec(nmix.shape, (0, 0)),
            _const_spec(win.shape, (0, 0)),
            _const_spec((None, CONV_K, cw), (l, 0, 0)),
            _const_spec(vnorm.shape, (0, 0)),
            _const_spec((None, n_heads, CHUNK, CHUNK), (l, 0, 0, 0)),
            _const_spec((None, CHUNK, sw), (l, 0, 0)),
            _const_spec(ona.shape, (0, 0)),
            _const_spec(onb.shape, (0, 0)),
            _const_spec(wout.shape, (0, 0)),
        ] + cast_in,
        out_specs=[pl.BlockSpec((tm, d), lambda i: (i, 0))] + cast_out,
        out_shape=[jax.ShapeDtypeStruct((n_tok, d), f32)] + cast_shapes,
        scratch_shapes=[
            pltpu.VMEM((SUBLANES, cw), f32),
            pltpu.VMEM((n_heads // 2, CHUNK, 2 * CHUNK), bf16),
        ],
        compiler_params=pltpu.CompilerParams(
            dimension_semantics=("arbitrary",),
            vmem_limit_bytes=VMEM_LIMIT_BYTES),
        name="mixer",
    )(x2d, mod, nmix, win, convw, vnorm, ws, bias, ona, onb, wout, *[w for w, _ in cast_jobs])


def _ffn_kernel(x_ref, mod_ref, nffn_ref, wup_ref, wdown_ref, nfin_ref, *refs,
                layer, final_norm, n_cast):
    cast_src, o_ref, cast_dst = refs[:n_cast], refs[n_cast], refs[n_cast + 1:]
    _run_cast_jobs(cast_src, cast_dst)
    d_ff = wdown_ref.shape[0]
    rg = ROW_GROUP
    groups = range(GROUPS_PER_STEP)
    sh = mod_ref[3:4, :]
    scale = nffn_ref[layer:layer + 1, :] * (1.0 + mod_ref[4:5, :])
    gate = mod_ref[5:6, :]
    xs = [x_ref[g * rg:(g + 1) * rg, :] for g in groups]
    hs = [(_rms(x) * scale + sh).astype(bf16) for x in xs]
    accs = [None for _ in groups]
    for c in range(d_ff // FF_CHUNK):
        lo = c * FF_CHUNK
        for g in groups:
            gt = _dot(hs[g], wup_ref[:, lo:lo + FF_CHUNK])
            up = _dot(hs[g], wup_ref[:, d_ff + lo:d_ff + lo + FF_CHUNK])
            a = (gt * jax.nn.sigmoid(gt) * up).astype(bf16)
            part = _dot(a, wdown_ref[lo:lo + FF_CHUNK, :])
            accs[g] = part if accs[g] is None else accs[g] + part
    for g in groups:
        y = xs[g] + gate * accs[g]
        if final_norm:
            y = _rms(y) * nfin_ref[...]
        o_ref[g * rg:(g + 1) * rg, :] = y


def _ffn(x2d, mod, nffn, wup, wdown, nfin, *, layer, seq, final_norm, cast_jobs=()):
    n_tok, d = x2d.shape
    tm = TOKEN_TILE
    tiles_per_seq = seq // tm
    l = layer
    cast_in, cast_out, cast_shapes = _cast_job_specs(cast_jobs, n_tok // tm)
    return pl.pallas_call(
        functools.partial(_ffn_kernel, layer=l, final_norm=final_norm, n_cast=len(cast_jobs)),
        grid=(n_tok // tm,),
        in_specs=[
            pl.BlockSpec((tm, d), lambda i: (i, 0)),
            pl.BlockSpec((None, None, N_MOD, d), lambda i: (l, i // tiles_per_seq, 0, 0)),
            _const_spec(nffn.shape, (0, 0)),
            _const_spec(wup.shape, (0, 0)),
            _const_spec(wdown.shape, (0, 0)),
            _const_spec((1, d), (0, 0)),
        ] + cast_in,
        out_specs=[pl.BlockSpec((tm, d), lambda i: (i, 0))] + cast_out,
        out_shape=[jax.ShapeDtypeStruct((n_tok, d), f32)] + cast_shapes,
        compiler_params=pltpu.CompilerParams(
            dimension_semantics=("arbitrary",),
            vmem_limit_bytes=VMEM_LIMIT_BYTES),
        name="ffn_final" if final_norm else "ffn",
    )(x2d, mod, nffn, wup, wdown, nfin, *[w for w, _ in cast_jobs])


def kernel(x, c, w_mod, b_mod, norm_mix, w_in, conv_w, v_norm, w_s, b_s,
           out_norm_a, out_norm_b, w_out, norm_ffn, w_up, w_down, norm_final):
    bsz, seq, d = x.shape
    depth = w_mod.shape[0]
    assert seq % TOKEN_TILE == 0 and ROW_GROUP % CHUNK == 0
    assert w_down.shape[1] % FF_CHUNK == 0

    mod = _modulation(c, w_mod, b_mod).reshape(depth, bsz, N_MOD, d)
    bias = jnp.repeat(jnp.swapaxes(b_s, 1, 2), HEAD_DIM, axis=2)
    win, wout = w_in[0].astype(bf16), w_out[0].astype(bf16)

    xt = x.reshape(bsz * seq, d)
    for l in range(depth):
        xt, wup, wdown = _mixer(xt, mod, norm_mix, win, conv_w, v_norm, w_s, bias, out_norm_a,
                                out_norm_b, wout,
                                layer=l, seq=seq, cast_jobs=((w_up, l), (w_down, l)))
        nxt = ((w_in, l + 1), (w_out, l + 1)) if l + 1 < depth else ()
        xt, *casts = _ffn(xt, mod, norm_ffn, wup, wdown, norm_final.reshape(1, d),
                          layer=l, seq=seq, final_norm=(l == depth - 1), cast_jobs=nxt)
        if casts:
            win, wout = casts
    return xt.reshape(bsz, seq, d)
```

```python
import functools

import jax
import jax.numpy as jnp
from jax import lax
from jax.experimental import pallas as pl
from jax.experimental.pallas import tpu as pltpu

EPS = 1e-6
GELU_K = 0.7978845608028654
GELU_C = 0.044715
HEAD_DIM = 64
CHUNK = 128
CONV_K = 3
N_MOD = 6
LANES = 128
SUBLANES = 8
BF16_TILE_ROWS = 16
ROW_GROUP = 512
GROUPS_PER_STEP = 2
TOKEN_TILE = ROW_GROUP * GROUPS_PER_STEP
MIXER_GROUPS = (ROW_GROUP,) * 4
MIXER_TILE = sum(MIXER_GROUPS)
MIXER_VMEM_LIMIT_BYTES = 62 * 1024 * 1024
FF_CHUNK = 256
VMEM_LIMIT_BYTES = 56 * 1024 * 1024

f32 = jnp.float32
bf16 = jnp.bfloat16


def _rms(x):
    return x * lax.rsqrt(jnp.mean(x * x, axis=-1, keepdims=True) + EPS)


def _dot(a, b):
    return jnp.dot(a, b, preferred_element_type=f32)


def _gelu(x):
    inner = x * (GELU_K + (GELU_K * GELU_C) * (x * x))
    half = 0.5 * x
    return half + half * jnp.tanh(inner)


def _const_spec(shape, index):
    return pl.BlockSpec(shape, lambda i: index, pipeline_mode=pl.Buffered(1))


def _cast_job_specs(jobs, n_steps):
    in_specs, out_specs, out_shapes = [], [], []
    for w, layer in jobs:
        _, r, c = w.shape
        span = next(k for k in (1, 2, 4, 8)
                    if n_steps % k == 0 and (r * k) % (n_steps * BF16_TILE_ROWS) == 0)
        rows = r * span // n_steps
        in_specs.append(pl.BlockSpec((None, rows, c), lambda i, l=layer, k=span: (l, i // k, 0)))
        out_specs.append(pl.BlockSpec((rows, c), lambda i, k=span: (i // k, 0)))
        out_shapes.append(jax.ShapeDtypeStruct((r, c), bf16))
    return in_specs, out_specs, out_shapes


def _run_cast_jobs(src_refs, dst_refs):
    for src, dst in zip(src_refs, dst_refs):
        dst[...] = src[...].astype(bf16)


def _mod_kernel(c_ref, w_ref, b_ref, o_ref):
    c = c_ref[...]
    bias = b_ref[pl.ds(pl.program_id(0), 1), :]
    o_ref[...] = _dot(c * jax.nn.sigmoid(c), w_ref[...]) + bias


def _modulation(c, w_mod, b_mod):
    depth, d, n = w_mod.shape
    bsz = c.shape[0]
    nblk = n // d
    return pl.pallas_call(
        _mod_kernel,
        grid=(depth, nblk),
        in_specs=[
            pl.BlockSpec((bsz, d), lambda l, j: (0, 0)),
            pl.BlockSpec((None, d, d), lambda l, j: (l, 0, j)),
            pl.BlockSpec((depth, d), lambda l, j: (0, j)),
        ],
        out_specs=pl.BlockSpec((None, bsz, d), lambda l, j: (l, 0, j)),
        out_shape=jax.ShapeDtypeStruct((depth, bsz, n), f32),
        compiler_params=pltpu.CompilerParams(
            dimension_semantics=("arbitrary", "arbitrary")),
        name="modulation",
    )(c, w_mod, b_mod)


def _mixer_kernel(x_ref, mod_ref, nmix_ref, win_ref, convw_ref, vnorm_ref,
                  ws_ref, bias_ref, ona_ref, onb_ref, wout_ref, *refs,
                  layer, tiles_per_seq, n_cast):
    cast_src, o_ref, cast_dst = refs[:n_cast], refs[n_cast], refs[n_cast + 1:2 * n_cast + 1]
    carry_ref, wsk_ref = refs[2 * n_cast + 1:]
    _run_cast_jobs(cast_src, cast_dst)
    i = pl.program_id(0)
    cw = convw_ref.shape[1]
    sw = vnorm_ref.shape[1]
    n_pairs = wsk_ref.shape[0]

    @pl.when(i == 0)
    def _():
        r = lax.broadcasted_iota(jnp.int32, (CHUNK, CHUNK), 0)
        col = lax.broadcasted_iota(jnp.int32, (CHUNK, CHUNK), 1)
        tril = col <= r
        for p in range(n_pairs):
            a = jnp.where(tril, ws_ref[2 * p], 0.0)
            b = jnp.where(tril, ws_ref[2 * p + 1], 0.0)
            wsk_ref[p] = jnp.concatenate([a, b], axis=1).astype(bf16)

    @pl.when(i % tiles_per_seq == 0)
    def _():
        carry_ref[...] = jnp.zeros_like(carry_ref)

    vnorm = vnorm_ref[layer:layer + 1, :]
    ona = ona_ref[layer:layer + 1, :]
    onb = onb_ref[layer:layer + 1, :]
    sh = mod_ref[0:1, :]
    scale = nmix_ref[layer:layer + 1, :] * (1.0 + mod_ref[1:2, :])
    gate = mod_ref[2:3, :]
    w0 = convw_ref[0:1, :]
    w1 = convw_ref[1:2, :]
    w2 = convw_ref[2:3, :]
    row8 = lax.broadcasted_iota(jnp.int32, (SUBLANES, cw), 0)
    lo_mask = lax.broadcasted_iota(jnp.int32, (CHUNK, LANES), 1) < HEAD_DIM
    zero = jnp.zeros((CHUNK, LANES), bf16)

    def project(x):
        h = _rms(x) * scale + sh
        return _dot(h.astype(bf16), win_ref[...])

    def conv_branch(proj, prev):
        bg = proj[:, 0:cw]
        z = proj[:, cw:2 * cw] * proj[:, 2 * cw:3 * cw]
        r1 = pltpu.roll(z, 1, 0)
        r2 = pltpu.roll(z, 2, 0)
        conv = w2 * z + w1 * r1 + w0 * r2
        z1h = jnp.where(row8 < 1, pltpu.roll(prev, 1, 0), r1[:SUBLANES])
        z2h = jnp.where(row8 < 2, pltpu.roll(prev, 2, 0), r2[:SUBLANES])
        conv_head = w2 * z[:SUBLANES] + w1 * z1h + w0 * z2h
        conv = jnp.concatenate([conv_head, conv[SUBLANES:]], axis=0)
        y_a = _rms(bg * conv) * ona
        return y_a.astype(bf16), z[z.shape[0] - SUBLANES:, :]

    def sgu_branch(proj):
        ug = _gelu(proj[:, 3 * cw:3 * cw + sw])
        vn = (_rms(_gelu(proj[:, 3 * cw + sw:])) * vnorm).astype(bf16)
        bias = bias_ref[...]
        mixed_rows = []
        for c in range(proj.shape[0] // CHUNK):
            blocks = []
            for p in range(n_pairs):
                vb = vn[c * CHUNK:(c + 1) * CHUNK, p * LANES:(p + 1) * LANES]
                rhs = jnp.concatenate(
                    [jnp.where(lo_mask, vb, zero), jnp.where(lo_mask, zero, vb)], axis=0)
                blocks.append(_dot(wsk_ref[p], rhs))
            mixed_rows.append(jnp.concatenate(blocks, axis=1) + bias)
        mixed = jnp.concatenate(mixed_rows, axis=0)
        return (_rms(ug * mixed) * onb).astype(bf16)

    starts =[sum(MIXER_GROUPS[:g]) for g in range(len(MIXER_GROUPS))]
    prev = carry_ref[...]
    xs = [x_ref[r0:r0 + n, :] for r0, n in zip(starts, MIXER_GROUPS)]
    projs = [project(x) for x in xs]
    for x, proj, r0, n in zip(xs, projs, starts, MIXER_GROUPS):
        y_b = sgu_branch(proj)
        y_a, prev = conv_branch(proj, prev)
        y = _dot(jnp.concatenate([y_a, y_b], axis=1), wout_ref[...])
        o_ref[r0:r0 + n, :] = x + gate * y
    carry_ref[...] = prev


def _mixer(x2d, mod, nmix, win, convw, vnorm, ws, bias, ona, onb, wout, *, layer, seq,
           cast_jobs=()):
    n_tok, d = x2d.shape
    tm = MIXER_TILE
    tiles_per_seq = seq // tm
    cast_in, cast_out, cast_shapes = _cast_job_specs(cast_jobs, n_tok // tm)
    cw = convw.shape[-1]
    sw = vnorm.shape[-1]
    n_heads = ws.shape[1]
    l = layer
    return pl.pallas_call(
        functools.partial(_mixer_kernel, layer=l, tiles_per_seq=tiles_per_seq,
                          n_cast=len(cast_jobs)),
        grid=(n_tok // tm,),
        in_specs=[
            pl.BlockSpec((tm, d), lambda i: (i, 0)),
            pl.BlockSpec((None, None, N_MOD, d), lambda i: (l, i // tiles_per_seq, 0, 0)),
            _const_spec(nmix.shape, (0, 0)),
            _const_spec(win.shape, (0, 0)),
            _const_spec((None, CONV_K, cw), (l, 0, 0)),
            _const_spec(vnorm.shape, (0, 0)),
            _const_spec((None, n_heads, CHUNK, CHUNK), (l, 0, 0, 0)),
            _const_spec((None, CHUNK, sw), (l, 0, 0)),
            _const_spec(ona.shape, (0, 0)),
            _const_spec(onb.shape, (0, 0)),
            _const_spec(wout.shape, (0, 0)),
        ] + cast_in,
        out_specs=[pl.BlockSpec((tm, d), lambda i: (i, 0))] + cast_out,
        out_shape=[jax.ShapeDtypeStruct((n_tok, d), f32)] + cast_shapes,
        scratch_shapes=[
            pltpu.VMEM((SUBLANES, cw), f32),
            pltpu.VMEM((n_heads // 2, CHUNK, 2 * CHUNK), bf16),
        ],
        compiler_params=pltpu.CompilerParams(
            dimension_semantics=("arbitrary",),
            vmem_limit_bytes=MIXER_VMEM_LIMIT_BYTES),
        name="mixer",
    )(x2d, mod, nmix, win, convw, vnorm, ws, bias, ona, onb, wout, *[w for w, _ in cast_jobs])


def _ffn_kernel(x_ref, mod_ref, nffn_ref, wup_ref, wdown_ref, nfin_ref, *refs,
                layer, final_norm, n_cast):
    cast_src, o_ref, cast_dst = refs[:n_cast], refs[n_cast], refs[n_cast + 1:]
    _run_cast_jobs(cast_src, cast_dst)
    d_ff = wdown_ref.shape[0]
    rg = ROW_GROUP
    groups = range(GROUPS_PER_STEP)
    sh = mod_ref[3:4, :]
    scale = nffn_ref[layer:layer + 1, :] * (1.0 + mod_ref[4:5, :])
    gate = mod_ref[5:6, :]
    xs = [x_ref[g * rg:(g + 1) * rg, :] for g in groups]
    hs = [(_rms(x) * scale + sh).astype(bf16) for x in xs]
    accs = [None for _ in groups]
    for c in range(d_ff // FF_CHUNK):
        lo = c * FF_CHUNK
        for g in groups:
            gt = _dot(hs[g], wup_ref[:, lo:lo + FF_CHUNK])
            up = _dot(hs[g], wup_ref[:, d_ff + lo:d_ff + lo + FF_CHUNK])
            a = (gt * jax.nn.sigmoid(gt) * up).astype(bf16)
            part = _dot(a, wdown_ref[lo:lo + FF_CHUNK, :])
            accs[g] = part if accs[g] is None else accs[g] + part
    for g in groups:
        y = xs[g] + gate * accs[g]
        if final_norm:
            y = _rms(y) * nfin_ref[...]
        o_ref[g * rg:(g + 1) * rg, :] = y


def _ffn(x2d, mod, nffn, wup, wdown, nfin, *, layer, seq, final_norm, cast_jobs=()):
    n_tok, d = x2d.shape
    tm = TOKEN_TILE
    tiles_per_seq = seq // tm
    l = layer
    cast_in, cast_out, cast_shapes = _cast_job_specs(cast_jobs, n_tok // tm)
    return pl.pallas_call(
        functools.partial(_ffn_kernel, layer=l, final_norm=final_norm, n_cast=len(cast_jobs)),
        grid=(n_tok // tm,),
        in_specs=[
            pl.BlockSpec((tm, d), lambda i: (i, 0)),
            pl.BlockSpec((None, None, N_MOD, d), lambda i: (l, i // tiles_per_seq, 0, 0)),
            _const_spec(nffn.shape, (0, 0)),
            _const_spec(wup.shape, (0, 0)),
            _const_spec(wdown.shape, (0, 0)),
            _const_spec((1, d), (0, 0)),
        ] + cast_in,
        out_specs=[pl.BlockSpec((tm, d), lambda i: (i, 0))] + cast_out,
        out_shape=[jax.ShapeDtypeStruct((n_tok, d), f32)] + cast_shapes,
        compiler_params=pltpu.CompilerParams(
            dimension_semantics=("arbitrary",),
            vmem_limit_bytes=VMEM_LIMIT_BYTES),
        name="ffn_final" if final_norm else "ffn",
    )(x2d, mod, nffn, wup, wdown, nfin, *[w for w, _ in cast_jobs])


def kernel(x, c, w_mod, b_mod, norm_mix, w_in, conv_w, v_norm, w_s, b_s,
           out_norm_a, out_norm_b, w_out, norm_ffn, w_up, w_down, norm_final):
    bsz, seq, d = x.shape
    depth = w_mod.shape[0]
    assert seq % TOKEN_TILE == 0 and seq % MIXER_TILE == 0 and ROW_GROUP % CHUNK == 0
    assert w_down.shape[1] % FF_CHUNK == 0

    mod = _modulation(c, w_mod, b_mod).reshape(depth, bsz, N_MOD, d)
    bias = jnp.repeat(jnp.swapaxes(b_s, 1, 2), HEAD_DIM, axis=2)
    win, wout = w_in[0].astype(bf16), w_out[0].astype(bf16)

    xt = x.reshape(bsz * seq, d)
    for l in range(depth):
        xt, wup, wdown = _mixer(xt, mod, norm_mix, win, conv_w, v_norm, w_s, bias, out_norm_a,
                                out_norm_b, wout,
                                layer=l, seq=seq, cast_jobs=((w_up, l), (w_down, l)))
        nxt = ((w_in, l + 1), (w_out, l + 1)) if l + 1 < depth else ()
        xt, *casts = _ffn(xt, mod, norm_ffn, wup, wdown, norm_final.reshape(1, d),
                          layer=l, seq=seq, final_norm=(l == depth - 1), cast_jobs=nxt)
        if casts:
            win, wout = casts
    return xt.reshape(bsz, seq, d)
```

```python
import functools

import jax
import jax.numpy as jnp
from jax import lax
from jax.experimental import pallas as pl
from jax.experimental.pallas import tpu as pltpu

EPS = 1e-6
GELU_K = 0.7978845608028654
GELU_C = 0.044715
HEAD_DIM = 64
CHUNK = 128
CONV_K = 3
N_MOD = 6
LANES = 128
SUBLANES = 8
BF16_TILE_ROWS = 16
ROW_GROUP = 512
GROUPS_PER_STEP = 2
TOKEN_TILE = ROW_GROUP * GROUPS_PER_STEP
MIXER_GROUPS = (ROW_GROUP,) * 4
MIXER_TILE = sum(MIXER_GROUPS)
MIXER_VMEM_LIMIT_BYTES = 62 * 1024 * 1024
FF_CHUNK = 256
MOD_BLOCK_COLS = 3072
VMEM_LIMIT_BYTES = 56 * 1024 * 1024

f32 = jnp.float32
bf16 = jnp.bfloat16


def _rms(x):
    return x * lax.rsqrt(jnp.mean(x * x, axis=-1, keepdims=True) + EPS)


def _dot(a, b):
    return jnp.dot(a, b, preferred_element_type=f32)


def _gelu(x):
    inner = x * (GELU_K + (GELU_K * GELU_C) * (x * x))
    half = 0.5 * x
    return half + half * jnp.tanh(inner)


def _const_spec(shape, index):
    return pl.BlockSpec(shape, lambda i: index, pipeline_mode=pl.Buffered(1))


def _cast_job_specs(jobs, n_steps, step=lambda i: i):
    in_specs, out_specs, out_shapes = [], [], []
    for w, layer in jobs:
        _, r, c = w.shape
        span = next(k for k in (1, 2, 4, 8)
                    if n_steps % k == 0 and (r * k) % (n_steps * BF16_TILE_ROWS) == 0)
        rows = r * span // n_steps
        in_specs.append(pl.BlockSpec((None, rows, c),
                                     lambda *ids, l=layer, k=span: (l, step(*ids) // k, 0)))
        out_specs.append(pl.BlockSpec((rows, c), lambda *ids, k=span: (step(*ids) // k, 0)))
        out_shapes.append(jax.ShapeDtypeStruct((r, c), bf16))
    return in_specs, out_specs, out_shapes


def _run_cast_jobs(src_refs, dst_refs):
    for src, dst in zip(src_refs, dst_refs):
        dst[...] = src[...].astype(bf16)


def _mod_kernel(c_ref, w_ref, b_ref, *refs, n_cast):
    cast_src, o_ref, cast_dst = refs[:n_cast], refs[n_cast], refs[n_cast + 1:]
    _run_cast_jobs(cast_src, cast_dst)
    c = c_ref[...]
    bias = b_ref[pl.ds(pl.program_id(0), 1), :]
    o_ref[...] = _dot(c * jax.nn.sigmoid(c), w_ref[...]) + bias


def _modulation(c, w_mod, b_mod, cast_jobs=()):
    depth, d, n = w_mod.shape
    bsz = c.shape[0]
    bc = MOD_BLOCK_COLS
    nblk = n // bc
    cast_in, cast_out, cast_shapes = _cast_job_specs(
        cast_jobs, depth * nblk, step=lambda l, j: l * nblk + j)
    return pl.pallas_call(
        functools.partial(_mod_kernel, n_cast=len(cast_jobs)),
        grid=(depth, nblk),
        in_specs=[
            pl.BlockSpec((bsz, d), lambda l, j: (0, 0)),
            pl.BlockSpec((None, d, bc), lambda l, j: (l, 0, j)),
            pl.BlockSpec((depth, bc), lambda l, j: (0, j)),
        ] + cast_in,
        out_specs=[pl.BlockSpec((None, bsz, bc), lambda l, j: (l, 0, j))] + cast_out,
        out_shape=[jax.ShapeDtypeStruct((depth, bsz, n), f32)] + cast_shapes,
        compiler_params=pltpu.CompilerParams(
            dimension_semantics=("arbitrary", "arbitrary"),
            vmem_limit_bytes=VMEM_LIMIT_BYTES),
        name="modulation",
    )(c, w_mod, b_mod, *[w for w, _ in cast_jobs])


def _mixer_kernel(x_ref, mod_ref, nmix_ref, win_ref, convw_ref, vnorm_ref,
                  ws_ref, bias_ref, ona_ref, onb_ref, wout_ref, *refs,
                  layer, tiles_per_seq, n_cast):
    cast_src, o_ref, cast_dst = refs[:n_cast], refs[n_cast], refs[n_cast + 1:2 * n_cast + 1]
    carry_ref, wsk_ref = refs[2 * n_cast + 1:]
    _run_cast_jobs(cast_src, cast_dst)
    i = pl.program_id(0)
    cw = convw_ref.shape[1]
    sw = vnorm_ref.shape[1]
    n_pairs = wsk_ref.shape[0]

    @pl.when(i == 0)
    def _():
        r = lax.broadcasted_iota(jnp.int32, (CHUNK, CHUNK), 0)
        col = lax.broadcasted_iota(jnp.int32, (CHUNK, CHUNK), 1)
        tril = col <= r
        for p in range(n_pairs):
            a = jnp.where(tril, ws_ref[2 * p], 0.0)
            b = jnp.where(tril, ws_ref[2 * p + 1], 0.0)
            wsk_ref[p] = jnp.concatenate([a, b], axis=1).astype(bf16)

    @pl.when(i % tiles_per_seq == 0)
    def _():
        carry_ref[...] = jnp.zeros_like(carry_ref)

    vnorm = vnorm_ref[layer:layer + 1, :]
    ona = ona_ref[layer:layer + 1, :]
    onb = onb_ref[layer:layer + 1, :]
    sh = mod_ref[0:1, :]
    scale = nmix_ref[layer:layer + 1, :] * (1.0 + mod_ref[1:2, :])
    gate = mod_ref[2:3, :]
    w0 = convw_ref[0:1, :]
    w1 = convw_ref[1:2, :]
    w2 = convw_ref[2:3, :]
    row8 = lax.broadcasted_iota(jnp.int32, (SUBLANES, cw), 0)
    lo_mask = lax.broadcasted_iota(jnp.int32, (CHUNK, LANES), 1) < HEAD_DIM
    zero = jnp.zeros((CHUNK, LANES), bf16)

    def project(x):
        h = _rms(x) * scale + sh
        return _dot(h.astype(bf16), win_ref[...])

    def conv_branch(proj, prev):
        bg = proj[:, 0:cw]
        z = proj[:, cw:2 * cw] * proj[:, 2 * cw:3 * cw]
        r1 = pltpu.roll(z, 1, 0)
        r2 = pltpu.roll(z, 2, 0)
        conv = w2 * z + w1 * r1 + w0 * r2
        z1h = jnp.where(row8 < 1, pltpu.roll(prev, 1, 0), r1[:SUBLANES])
        z2h = jnp.where(row8 < 2, pltpu.roll(prev, 2, 0), r2[:SUBLANES])
        conv_head = w2 * z[:SUBLANES] + w1 * z1h + w0 * z2h
        conv = jnp.concatenate([conv_head, conv[SUBLANES:]], axis=0)
        y_a = _rms(bg * conv) * ona
        return y_a.astype(bf16), z[z.shape[0] - SUBLANES:, :]

    def sgu_branch(proj):
        ug = _gelu(proj[:, 3 * cw:3 * cw + sw])
        vn = (_rms(_gelu(proj[:, 3 * cw + sw:])) * vnorm).astype(bf16)
        bias = bias_ref[...]
        mixed_rows = []
        for c in range(proj.shape[0] // CHUNK):
            blocks = []
            for p in range(n_pairs):
                vb = vn[c * CHUNK:(c + 1) * CHUNK, p * LANES:(p + 1) * LANES]
                rhs = jnp.concatenate(
                    [jnp.where(lo_mask, vb, zero), jnp.where(lo_mask, zero, vb)], axis=0)
                blocks.append(_dot(wsk_ref[p], rhs))
            mixed_rows.append(jnp.concatenate(blocks, axis=1) + bias)
        mixed = jnp.concatenate(mixed_rows, axis=0)
        return (_rms(ug * mixed) * onb).astype(bf16)

    starts =[sum(MIXER_GROUPS[:g]) for g in range(len(MIXER_GROUPS))]
    prev = carry_ref[...]
    xs = [x_ref[r0:r0 + n, :] for r0, n in zip(starts, MIXER_GROUPS)]
    projs = [project(x) for x in xs]
    for x, proj, r0, n in zip(xs, projs, starts, MIXER_GROUPS):
        y_b = sgu_branch(proj)
        y_a, prev = conv_branch(proj, prev)
        y = _dot(jnp.concatenate([y_a, y_b], axis=1), wout_ref[...])
        o_ref[r0:r0 + n, :] = x + gate * y
    carry_ref[...] = prev


def _mixer(x2d, mod, nmix, win, convw, vnorm, ws, bias, ona, onb, wout, *, layer, seq,
           cast_jobs=()):
    n_tok, d = x2d.shape
    tm = MIXER_TILE
    tiles_per_seq = seq // tm
    cast_in, cast_out, cast_shapes = _cast_job_specs(cast_jobs, n_tok // tm)
    cw = convw.shape[-1]
    sw = vnorm.shape[-1]
    n_heads = ws.shape[1]
    l = layer
    return pl.pallas_call(
        functools.partial(_mixer_kernel, layer=l, tiles_per_seq=tiles_per_seq,
                          n_cast=len(cast_jobs)),
        grid=(n_tok // tm,),
        in_specs=[
            pl.BlockSpec((tm, d), lambda i: (i, 0)),
            pl.BlockSpec((None, None, N_MOD, d), lambda i: (l, i // tiles_per_seq, 0, 0)),
            _const_spec(nmix.shape, (0, 0)),
            _const_spec(win.shape, (0, 0)),
            _const_spec((None, CONV_K, cw), (l, 0, 0)),
            _const_spec(vnorm.shape, (0, 0)),
            _const_spec((None, n_heads, CHUNK, CHUNK), (l, 0, 0, 0)),
            _const_spec((None, CHUNK, sw), (l, 0, 0)),
            _const_spec(ona.shape, (0, 0)),
            _const_spec(onb.shape, (0, 0)),
            _const_spec(wout.shape, (0, 0)),
        ] + cast_in,
        out_specs=[pl.BlockSpec((tm, d), lambda i: (i, 0))] + cast_out,
        out_shape=[jax.ShapeDtypeStruct((n_tok, d), f32)] + cast_shapes,
        scratch_shapes=[
            pltpu.VMEM((SUBLANES, cw), f32),
            pltpu.VMEM((n_heads // 2, CHUNK, 2 * CHUNK), bf16),
        ],
        compiler_params=pltpu.CompilerParams(
            dimension_semantics=("arbitrary",),
            vmem_limit_bytes=MIXER_VMEM_LIMIT_BYTES),
        name="mixer",
    )(x2d, mod, nmix, win, convw, vnorm, ws, bias, ona, onb, wout, *[w for w, _ in cast_jobs])


def _ffn_kernel(x_ref, mod_ref, nffn_ref, wup_ref, wdown_ref, nfin_ref, *refs,
                layer, final_norm, n_cast):
    cast_src, o_ref, cast_dst = refs[:n_cast], refs[n_cast], refs[n_cast + 1:]
    _run_cast_jobs(cast_src, cast_dst)
    d_ff = wdown_ref.shape[0]
    rg = ROW_GROUP
    groups = range(GROUPS_PER_STEP)
    sh = mod_ref[3:4, :]
    scale = nffn_ref[layer:layer + 1, :] * (1.0 + mod_ref[4:5, :])
    gate = mod_ref[5:6, :]
    xs = [x_ref[g * rg:(g + 1) * rg, :] for g in groups]
    hs = [(_rms(x) * scale + sh).astype(bf16) for x in xs]
    accs = [None for _ in groups]
    for c in range(d_ff // FF_CHUNK):
        lo = c * FF_CHUNK
        for g in groups:
            gt = _dot(hs[g], wup_ref[:, lo:lo + FF_CHUNK])
            up = _dot(hs[g], wup_ref[:, d_ff + lo:d_ff + lo + FF_CHUNK])
            a = (gt * jax.nn.sigmoid(gt) * up).astype(bf16)
            part = _dot(a, wdown_ref[lo:lo + FF_CHUNK, :])
            accs[g] = part if accs[g] is None else accs[g] + part
    for g in groups:
        y = xs[g] + gate * accs[g]
        if final_norm:
            y = _rms(y) * nfin_ref[...]
        o_ref[g * rg:(g + 1) * rg, :] = y


def _ffn(x2d, mod, nffn, wup, wdown, nfin, *, layer, seq, final_norm, cast_jobs=()):
    n_tok, d = x2d.shape
    tm = TOKEN_TILE
    tiles_per_seq = seq // tm
    l = layer
    cast_in, cast_out, cast_shapes = _cast_job_specs(cast_jobs, n_tok // tm)
    return pl.pallas_call(
        functools.partial(_ffn_kernel, layer=l, final_norm=final_norm, n_cast=len(cast_jobs)),
        grid=(n_tok // tm,),
        in_specs=[
            pl.BlockSpec((tm, d), lambda i: (i, 0)),
            pl.BlockSpec((None, None, N_MOD, d), lambda i: (l, i // tiles_per_seq, 0, 0)),
            _const_spec(nffn.shape, (0, 0)),
            _const_spec(wup.shape, (0, 0)),
            _const_spec(wdown.shape, (0, 0)),
            _const_spec((1, d), (0, 0)),
        ] + cast_in,
        out_specs=[pl.BlockSpec((tm, d), lambda i: (i, 0))] + cast_out,
        out_shape=[jax.ShapeDtypeStruct((n_tok, d), f32)] + cast_shapes,
        compiler_params=pltpu.CompilerParams(
            dimension_semantics=("arbitrary",),
            vmem_limit_bytes=VMEM_LIMIT_BYTES),
        name="ffn_final" if final_norm else "ffn",
    )(x2d, mod, nffn, wup, wdown, nfin, *[w for w, _ in cast_jobs])


def kernel(x, c, w_mod, b_mod, norm_mix, w_in, conv_w, v_norm, w_s, b_s,
           out_norm_a, out_norm_b, w_out, norm_ffn, w_up, w_down, norm_final):
    bsz, seq, d = x.shape
    depth = w_mod.shape[0]
    assert seq % TOKEN_TILE == 0 and seq % MIXER_TILE == 0 and ROW_GROUP % CHUNK == 0
    assert w_down.shape[1] % FF_CHUNK == 0 and w_mod.shape[2] % MOD_BLOCK_COLS == 0

    mod, win, wout = _modulation(c, w_mod, b_mod, cast_jobs=((w_in, 0), (w_out, 0)))
    mod = mod.reshape(depth, bsz, N_MOD, d)
    bias = jnp.repeat(jnp.swapaxes(b_s, 1, 2), HEAD_DIM, axis=2)

    xt = x.reshape(bsz * seq, d)
    for l in range(depth):
        xt, wup, wdown = _mixer(xt, mod, norm_mix, win, conv_w, v_norm, w_s, bias, out_norm_a,
                                out_norm_b, wout,
                                layer=l, seq=seq, cast_jobs=((w_up, l), (w_down, l)))
        nxt = ((w_in, l + 1), (w_out, l + 1)) if l + 1 < depth else ()
        xt, *casts = _ffn(xt, mod, norm_ffn, wup, wdown, norm_final.reshape(1, d),
                          layer=l, seq=seq, final_norm=(l == depth - 1), cast_jobs=nxt)
        if casts:
            win, wout = casts
    return xt.reshape(bsz, seq, d)
```

```python
import functools

import jax
import jax.numpy as jnp
from jax import lax
from jax.experimental import pallas as pl
from jax.experimental.pallas import tpu as pltpu

EPS = 1e-6
GELU_K = 0.7978845608028654
GELU_C = 0.044715
HEAD_DIM = 64
CHUNK = 128
CONV_K = 3
N_MOD = 6
LANES = 128
SUBLANES = 8
BF16_TILE_ROWS = 16
ROW_GROUP = 512
GROUPS_PER_STEP = 2
TOKEN_TILE = ROW_GROUP * GROUPS_PER_STEP
MIXER_GROUPS = (ROW_GROUP,) * 2
MIXER_TILE = sum(MIXER_GROUPS)
MIXER_VMEM_LIMIT_BYTES = 62 * 1024 * 1024
FF_CHUNK = 256
MOD_BLOCK_COLS = 3072
VMEM_LIMIT_BYTES = 56 * 1024 * 1024

f32 = jnp.float32
bf16 = jnp.bfloat16


def _rms(x):
    return x * lax.rsqrt(jnp.mean(x * x, axis=-1, keepdims=True) + EPS)


def _dot(a, b):
    return jnp.dot(a, b, preferred_element_type=f32)


def _gelu(x):
    inner = x * (GELU_K + (GELU_K * GELU_C) * (x * x))
    half = 0.5 * x
    return half + half * jnp.tanh(inner)


def _const_spec(shape, index):
    return pl.BlockSpec(shape, lambda i: index, pipeline_mode=pl.Buffered(1))


def _cast_job_specs(jobs, n_steps, step=lambda i: i):
    in_specs, out_specs, out_shapes = [], [], []
    for w, layer in jobs:
        _, r, c = w.shape
        span = next(k for k in (1, 2, 4, 8)
                    if n_steps % k == 0 and (r * k) % (n_steps * BF16_TILE_ROWS) == 0)
        rows = r * span // n_steps
        in_specs.append(pl.BlockSpec((None, rows, c),
                                     lambda *ids, l=layer, k=span: (l, step(*ids) // k, 0)))
        out_specs.append(pl.BlockSpec((rows, c), lambda *ids, k=span: (step(*ids) // k, 0)))
        out_shapes.append(jax.ShapeDtypeStruct((r, c), bf16))
    return in_specs, out_specs, out_shapes


def _run_cast_jobs(src_refs, dst_refs):
    for src, dst in zip(src_refs, dst_refs):
        dst[...] = src[...].astype(bf16)


def _mod_kernel(c_ref, w_ref, b_ref, *refs, n_cast):
    cast_src, o_ref, cast_dst = refs[:n_cast], refs[n_cast], refs[n_cast + 1:]
    _run_cast_jobs(cast_src, cast_dst)
    c = c_ref[...]
    bias = b_ref[pl.ds(pl.program_id(0), 1), :]
    o_ref[...] = _dot(c * jax.nn.sigmoid(c), w_ref[...]) + bias


def _modulation(c, w_mod, b_mod, cast_jobs=()):
    depth, d, n = w_mod.shape
    bsz = c.shape[0]
    bc = MOD_BLOCK_COLS
    nblk = n // bc
    cast_in, cast_out, cast_shapes = _cast_job_specs(
        cast_jobs, depth * nblk, step=lambda l, j: l * nblk + j)
    return pl.pallas_call(
        functools.partial(_mod_kernel, n_cast=len(cast_jobs)),
        grid=(depth, nblk),
        in_specs=[
            pl.BlockSpec((bsz, d), lambda l, j: (0, 0)),
            pl.BlockSpec((None, d, bc), lambda l, j: (l, 0, j)),
            pl.BlockSpec((depth, bc), lambda l, j: (0, j)),
        ] + cast_in,
        out_specs=[pl.BlockSpec((None, bsz, bc), lambda l, j: (l, 0, j))] + cast_out,
        out_shape=[jax.ShapeDtypeStruct((depth, bsz, n), f32)] + cast_shapes,
        compiler_params=pltpu.CompilerParams(
            dimension_semantics=("arbitrary", "arbitrary"),
            vmem_limit_bytes=VMEM_LIMIT_BYTES),
        name="modulation",
    )(c, w_mod, b_mod, *[w for w, _ in cast_jobs])


def _mixer_kernel(x_ref, mod_ref, nmix_ref, win_ref, convw_ref, vnorm_ref,
                  ws_ref, bias_ref, ona_ref, onb_ref, wout_ref, nffn_ref, *refs,
                  layer, tiles_per_seq, n_cast):
    cast_src, o_ref, h2_ref = refs[:n_cast], refs[n_cast], refs[n_cast + 1]
    cast_dst = refs[n_cast + 2:2 * n_cast + 2]
    carry_ref, wsk_ref = refs[2 * n_cast + 2:]
    _run_cast_jobs(cast_src, cast_dst)
    i = pl.program_id(0)
    cw = convw_ref.shape[1]
    sw = vnorm_ref.shape[1]
    n_pairs = wsk_ref.shape[0]

    @pl.when(i == 0)
    def _():
        r = lax.broadcasted_iota(jnp.int32, (CHUNK, CHUNK), 0)
        col = lax.broadcasted_iota(jnp.int32, (CHUNK, CHUNK), 1)
        tril = col <= r
        for p in range(n_pairs):
            a = jnp.where(tril, ws_ref[2 * p], 0.0)
            b = jnp.where(tril, ws_ref[2 * p + 1], 0.0)
            wsk_ref[p] = jnp.concatenate([a, b], axis=1).astype(bf16)

    @pl.when(i % tiles_per_seq == 0)
    def _():
        carry_ref[...] = jnp.zeros_like(carry_ref)

    vnorm = vnorm_ref[layer:layer + 1, :]
    ona = ona_ref[layer:layer + 1, :]
    onb = onb_ref[layer:layer + 1, :]
    sh = mod_ref[0:1, :]
    scale = nmix_ref[layer:layer + 1, :] * (1.0 + mod_ref[1:2, :])
    gate = mod_ref[2:3, :]
    w0 = convw_ref[0:1, :]
    w1 = convw_ref[1:2, :]
    w2 = convw_ref[2:3, :]
    row8 = lax.broadcasted_iota(jnp.int32, (SUBLANES, cw), 0)
    lo_mask = lax.broadcasted_iota(jnp.int32, (CHUNK, LANES), 1) < HEAD_DIM
    zero = jnp.zeros((CHUNK, LANES), bf16)

    def project(x):
        h = _rms(x) * scale + sh
        return _dot(h.astype(bf16), win_ref[...])

    def conv_branch(proj, prev):
        bg = proj[:, 0:cw]
        z = proj[:, cw:2 * cw] * proj[:, 2 * cw:3 * cw]
        r1 = pltpu.roll(z, 1, 0)
        r2 = pltpu.roll(z, 2, 0)
        conv = w2 * z + w1 * r1 + w0 * r2
        z1h = jnp.where(row8 < 1, pltpu.roll(prev, 1, 0), r1[:SUBLANES])
        z2h = jnp.where(row8 < 2, pltpu.roll(prev, 2, 0), r2[:SUBLANES])
        conv_head = w2 * z[:SUBLANES] + w1 * z1h + w0 * z2h
        conv = jnp.concatenate([conv_head, conv[SUBLANES:]], axis=0)
        y_a = _rms(bg * conv) * ona
        return y_a.astype(bf16), z[z.shape[0] - SUBLANES:, :]

    def sgu_branch(proj):
        ug = _gelu(proj[:, 3 * cw:3 * cw + sw])
        vn = (_rms(_gelu(proj[:, 3 * cw + sw:])) * vnorm).astype(bf16)
        bias = bias_ref[...]
        mixed_rows = []
        for c in range(proj.shape[0] // CHUNK):
            blocks = []
            for p in range(n_pairs):
                vb = vn[c * CHUNK:(c + 1) * CHUNK, p * LANES:(p + 1) * LANES]
                rhs = jnp.concatenate(
                    [jnp.where(lo_mask, vb, zero), jnp.where(lo_mask, zero, vb)], axis=0)
                blocks.append(_dot(wsk_ref[p], rhs))
            mixed_rows.append(jnp.concatenate(blocks, axis=1) + bias)
        mixed = jnp.concatenate(mixed_rows, axis=0)
        return (_rms(ug * mixed) * onb).astype(bf16)

    starts =[sum(MIXER_GROUPS[:g]) for g in range(len(MIXER_GROUPS))]
    prev = carry_ref[...]
    scale_f = nffn_ref[layer:layer + 1, :] * (1.0 + mod_ref[4:5, :])
    xs = [x_ref[r0:r0 + n, :] for r0, n in zip(starts, MIXER_GROUPS)]
    projs = [project(x) for x in xs]
    for x, proj, r0, n in zip(xs, projs, starts, MIXER_GROUPS):
        y_b = sgu_branch(proj)
        y_a, prev = conv_branch(proj, prev)
        y = _dot(jnp.concatenate([y_a, y_b], axis=1), wout_ref[...])
        x1 = x + gate * y
        o_ref[r0:r0 + n, :] = x1
        h2_ref[r0:r0 + n, :] = (_rms(x1) * scale_f + mod_ref[3:4, :]).astype(bf16)
    carry_ref[...] = prev


def _mixer(x2d, mod, nmix, win, convw, vnorm, ws, bias, ona, onb, wout, nffn, *, layer, seq,
           cast_jobs=()):
    n_tok, d = x2d.shape
    tm = MIXER_TILE
    tiles_per_seq = seq // tm
    cast_in, cast_out, cast_shapes = _cast_job_specs(cast_jobs, n_tok // tm)
    cw = convw.shape[-1]
    sw = vnorm.shape[-1]
    n_heads = ws.shape[1]
    l = layer
    return pl.pallas_call(
        functools.partial(_mixer_kernel, layer=l, tiles_per_seq=tiles_per_seq,
                          n_cast=len(cast_jobs)),
        grid=(n_tok // tm,),
        in_specs=[
            pl.BlockSpec((tm, d), lambda i: (i, 0)),
            pl.BlockSpec((None, None, N_MOD, d), lambda i: (l, i // tiles_per_seq, 0, 0)),
            _const_spec(nmix.shape, (0, 0)),
            _const_spec(win.shape, (0, 0)),
            _const_spec((None, CONV_K, cw), (l, 0, 0)),
            _const_spec(vnorm.shape, (0, 0)),
            _const_spec((None, n_heads, CHUNK, CHUNK), (l, 0, 0, 0)),
            _const_spec((None, CHUNK, sw), (l, 0, 0)),
            _const_spec(ona.shape, (0, 0)),
            _const_spec(onb.shape, (0, 0)),
            _const_spec(wout.shape, (0, 0)),
            _const_spec(nffn.shape, (0, 0)),
        ] + cast_in,
        out_specs=[pl.BlockSpec((tm, d), lambda i: (i, 0)),
                   pl.BlockSpec((tm, d), lambda i: (i, 0))] + cast_out,
        out_shape=[jax.ShapeDtypeStruct((n_tok, d), f32),
                   jax.ShapeDtypeStruct((n_tok, d), bf16)] + cast_shapes,
        scratch_shapes=[
            pltpu.VMEM((SUBLANES, cw), f32),
            pltpu.VMEM((n_heads // 2, CHUNK, 2 * CHUNK), bf16),
        ],
        compiler_params=pltpu.CompilerParams(
            dimension_semantics=("arbitrary",),
            vmem_limit_bytes=MIXER_VMEM_LIMIT_BYTES),
        name="mixer",
    )(x2d, mod, nmix, win, convw, vnorm, ws, bias, ona, onb, wout, nffn,
      *[w for w, _ in cast_jobs])


def _ffn_kernel(x_ref, h_ref, mod_ref, wup_ref, wdown_ref, nfin_ref, *refs,
                final_norm, n_cast):
    cast_src, o_ref, cast_dst = refs[:n_cast], refs[n_cast], refs[n_cast + 1:]
    _run_cast_jobs(cast_src, cast_dst)
    d_ff = wdown_ref.shape[0]
    rg = ROW_GROUP
    groups = range(GROUPS_PER_STEP)
    gate = mod_ref[5:6, :]
    xs = [x_ref[g * rg:(g + 1) * rg, :] for g in groups]
    hs = [h_ref[g * rg:(g + 1) * rg, :] for g in groups]
    accs = [None for _ in groups]
    for c in range(d_ff // FF_CHUNK):
        lo = c * FF_CHUNK
        for g in groups:
            gt = _dot(hs[g], wup_ref[:, lo:lo + FF_CHUNK])
            up = _dot(hs[g], wup_ref[:, d_ff + lo:d_ff + lo + FF_CHUNK])
            a = (gt * jax.nn.sigmoid(gt) * up).astype(bf16)
            part = _dot(a, wdown_ref[lo:lo + FF_CHUNK, :])
            accs[g] = part if accs[g] is None else accs[g] + part
    for g in groups:
        y = xs[g] + gate * accs[g]
        if final_norm:
            y = _rms(y) * nfin_ref[...]
        o_ref[g * rg:(g + 1) * rg, :] = y


def _ffn(x2d, h2d, mod, wup, wdown, nfin, *, layer, seq, final_norm, cast_jobs=()):
    n_tok, d = x2d.shape
    tm = TOKEN_TILE
    tiles_per_seq = seq // tm
    l = layer
    cast_in, cast_out, cast_shapes = _cast_job_specs(cast_jobs, n_tok // tm)
    return pl.pallas_call(
        functools.partial(_ffn_kernel, final_norm=final_norm, n_cast=len(cast_jobs)),
        grid=(n_tok // tm,),
        in_specs=[
            pl.BlockSpec((tm, d), lambda i: (i, 0)),
            pl.BlockSpec((tm, d), lambda i: (i, 0)),
            pl.BlockSpec((None, None, N_MOD, d), lambda i: (l, i // tiles_per_seq, 0, 0)),
            _const_spec(wup.shape, (0, 0)),
            _const_spec(wdown.shape, (0, 0)),
            _const_spec((1, d), (0, 0)),
        ] + cast_in,
        out_specs=[pl.BlockSpec((tm, d), lambda i: (i, 0))] + cast_out,
        out_shape=[jax.ShapeDtypeStruct((n_tok, d), f32)] + cast_shapes,
        compiler_params=pltpu.CompilerParams(
            dimension_semantics=("arbitrary",),
            vmem_limit_bytes=VMEM_LIMIT_BYTES),
        name="ffn_final" if final_norm else "ffn",
    )(x2d, h2d, mod, wup, wdown, nfin, *[w for w, _ in cast_jobs])


def kernel(x, c, w_mod, b_mod, norm_mix, w_in, conv_w, v_norm, w_s, b_s,
           out_norm_a, out_norm_b, w_out, norm_ffn, w_up, w_down, norm_final):
    bsz, seq, d = x.shape
    depth = w_mod.shape[0]
    assert seq % TOKEN_TILE == 0 and seq % MIXER_TILE == 0 and ROW_GROUP % CHUNK == 0
    assert w_down.shape[1] % FF_CHUNK == 0 and w_mod.shape[2] % MOD_BLOCK_COLS == 0

    mod, win, wout = _modulation(c, w_mod, b_mod, cast_jobs=((w_in, 0), (w_out, 0)))
    mod = mod.reshape(depth, bsz, N_MOD, d)
    bias = jnp.repeat(jnp.swapaxes(b_s, 1, 2), HEAD_DIM, axis=2)

    xt = x.reshape(bsz * seq, d)
    for l in range(depth):
        xt, ht, wup, wdown = _mixer(xt, mod, norm_mix, win, conv_w, v_norm, w_s, bias, out_norm_a,
                                    out_norm_b, wout, norm_ffn,
                                    layer=l, seq=seq, cast_jobs=((w_up, l), (w_down, l)))
        nxt = ((w_in, l + 1), (w_out, l + 1)) if l + 1 < depth else ()
        xt, *casts = _ffn(xt, ht, mod, wup, wdown, norm_final.reshape(1, d),
                          layer=l, seq=seq, final_norm=(l == depth - 1), cast_jobs=nxt)
        if casts:
            win, wout = casts
    return xt.reshape(bsz, seq, d)
```
